```python
import math
import jax, jax.numpy as jnp
from jax import lax
import numpy as np

D_MODEL = 4096
BATCH = 4
SEQ = 2048
DEPTH = 2
DEC_BATCH = 8
DEC_SEQ = 1
PAST_LEN = 16384
PAGE_SIZE = 128

W_A = D_MODEL // 4
W_B = D_MODEL // 4
HEAD_DIM = 128
N_ATTN_GROUPS = 3
HEADS_PER_GROUP = 4
N_ATTN_HEADS = N_ATTN_GROUPS * HEADS_PER_GROUP
W_C_QKV = N_ATTN_HEADS * HEAD_DIM
W_C = HEADS_PER_GROUP * HEAD_DIM
W_D = D_MODEL // 4
CHUNK = 128
D_GROUP_DIM = 128
D_GROUPS = W_D // D_GROUP_DIM
CONV_A = 3
CONV_B = 31
WINDOWS = (128, 512, 2048)
DILATIONS = (1, 4, 16)
ROPE_THETA = 10000.0
N_BRANCH = 4
LN_EPS = 1e-5
ALPHA = (2 * DEPTH) ** 0.25
BETA = (8 * DEPTH) ** -0.25
SPLIT_SIZES = (W_A, W_A, W_A, W_A,
               W_B, W_B, W_B,
               W_C_QKV, W_C_QKV, W_C_QKV, W_C,
               W_D, W_D, W_D,
               N_BRANCH * D_MODEL)
N_IN = sum(SPLIT_SIZES)

kernel_name = "hybrid_conv_dilated_gmlp_decoder_step"


def _split_cols(h):
    offs = []
    acc = 0
    for s in SPLIT_SIZES[:-1]:
        acc += s
        offs.append(acc)
    return jnp.split(h, offs, axis=-1)


def _layernorm(x, g, b):
    xf = x.astype(jnp.float32)
    mu = jnp.mean(xf, axis=-1, keepdims=True)
    var = jnp.mean(jnp.square(xf - mu), axis=-1, keepdims=True)
    return ((xf - mu) * lax.rsqrt(var + LN_EPS) * g.astype(jnp.float32)
            + b.astype(jnp.float32)).astype(x.dtype)


def _rope(x, pos):
    half = HEAD_DIM // 2
    inv = ROPE_THETA ** (-jnp.arange(half, dtype=jnp.float32) / half)
    ang = pos.astype(jnp.float32)[:, None] * inv[None, :]
    cos = jnp.cos(ang)[None, :, None, :]
    sin = jnp.sin(ang)[None, :, None, :]
    x1 = x[..., :half].astype(jnp.float32)
    x2 = x[..., half:].astype(jnp.float32)
    return jnp.concatenate([x1 * cos - x2 * sin, x2 * cos + x1 * sin], axis=-1).astype(x.dtype)


def _dwconv(u_ext, w):
    return lax.conv_general_dilated(
        u_ext, w.astype(u_ext.dtype)[:, None, :], window_strides=(1,), padding='VALID',
        dimension_numbers=('NWC', 'WIO', 'NWC'), feature_group_count=u_ext.shape[-1])


def _dilated_prompt(q, k, v, dil, span):
    nb_, s_len, nh, dh = q.shape
    blk = span
    unit = dil * blk
    sp = -(-s_len // unit) * unit
    m_len = sp // dil
    nblk = m_len // blk

    def split(a):
        a = jnp.pad(a, ((0, 0), (0, sp - s_len), (0, 0), (0, 0)))
        a = a.reshape(nb_, m_len, dil, nh, dh).transpose(0, 2, 1, 3, 4)
        return a.reshape(nb_, dil, nblk, blk, nh, dh)

    def with_prev(a):
        prev = jnp.pad(a, ((0, 0), (0, 0), (1, 0), (0, 0), (0, 0), (0, 0)))[:, :, :-1]
        return jnp.concatenate([prev, a], axis=3)

    qs = split(q)
    ks = with_prev(split(k))
    vs = with_prev(split(v))
    s = jnp.einsum('brnqhd,brnkhd->brnhqk', qs, ks,
                   preferred_element_type=jnp.float32) * (dh ** -0.5)
    i = jnp.arange(blk)[:, None]
    j = jnp.arange(2 * blk)[None, :]
    band = jnp.where(j < blk, j >= i, (j - blk) <= i)
    has_prev = (jnp.arange(nblk) > 0)[:, None, None]
    valid = band[None] & (has_prev | (j >= blk)[None])
    s = jnp.where(valid[None, None, :, None], s, -jnp.inf)
    mx = jnp.max(s, axis=-1, keepdims=True)
    p = jnp.exp(s - mx)
    den = jnp.sum(p, axis=-1, keepdims=True)
    o = jnp.einsum('brnhqk,brnkhd->brnqhd', (p / den).astype(v.dtype), vs)
    lse = (mx + jnp.log(den))[..., 0]
    o = o.reshape(nb_, dil, m_len, nh, dh).transpose(0, 2, 1, 3, 4).reshape(nb_, sp, nh, dh)[:, :s_len]
    lse = lse.transpose(0, 1, 2, 4, 3).reshape(nb_, dil, m_len, nh)
    lse = lse.transpose(0, 2, 1, 3).reshape(nb_, sp, nh)[:, :s_len]
    return o, lse


def _dilated_decode(q, k_ext, v_ext, n_hist, dil, span):
    t = q.shape[1]
    idx = n_hist + jnp.arange(t)[:, None] - dil * jnp.arange(span + 1)[None, :]
    valid = idx >= 0
    idx = jnp.maximum(idx, 0)
    kg = k_ext[:, idx]
    vg = v_ext[:, idx]
    s = jnp.einsum('bthd,btkhd->bthk', q, kg,
                   preferred_element_type=jnp.float32) * (q.shape[-1] ** -0.5)
    s = jnp.where(valid[None, :, None, :], s, -jnp.inf)
    mx = jnp.max(s, axis=-1, keepdims=True)
    p = jnp.exp(s - mx)
    den = jnp.sum(p, axis=-1, keepdims=True)
    o = jnp.einsum('bthk,btkhd->bthd', (p / den).astype(vg.dtype), vg)
    lse = (mx + jnp.log(den))[..., 0]
    return o, lse


def _chunk_mix(v, w_s, b_s):
    nb_, t, _ = v.shape
    tp = -(-t // CHUNK) * CHUNK
    vp = jnp.pad(v, ((0, 0), (0, tp - t), (0, 0))).reshape(nb_, tp // CHUNK, CHUNK, D_GROUPS, D_GROUP_DIM)
    causal = jnp.tril(jnp.ones((CHUNK, CHUNK), dtype=bool))
    wm = jnp.where(causal[None], w_s, 0.0).astype(v.dtype)
    out = jnp.einsum('gts,bcsgd->bctgd', wm, vp) + b_s.T.astype(v.dtype)[None, None, :, :, None]
    return out.reshape(nb_, tp, W_D)[:, :t]


def _layer(x, pos, hist_a, hist_b, attend, lw):
    (w_in, conv_a_w, conv_b_w, conv_b_bias, ln_b_g, ln_b_b, ln_d_g, ln_d_b, w_s, b_s,
     w_out_a, w_out_b, w_out_c, w_out_d, b_gate, w_o, ln_g, ln_b) = lw
    nb_, t, _ = x.shape
    h = jnp.einsum('btd,dn->btn', x, w_in)
    (a_b, a_c, a_h, a_z, b_a, b_g, b_z, c_q, c_k, c_v, c_z,
     d_u, d_v, d_z, g_log) = _split_cols(h)

    ua = jnp.concatenate([hist_a.astype(x.dtype), a_c * a_h], axis=1)
    y_a = a_b * _dwconv(ua, conv_a_w) * jax.nn.silu(a_z)

    ub = jnp.concatenate([hist_b.astype(x.dtype), b_a * jax.nn.sigmoid(b_g)], axis=1)
    cb = _dwconv(ub, conv_b_w) + conv_b_bias.astype(x.dtype)
    y_b = jax.nn.silu(_layernorm(cb, ln_b_g, ln_b_b)) * jax.nn.silu(b_z)

    q = _rope(c_q.reshape(nb_, t, N_ATTN_HEADS, HEAD_DIM), pos)
    k = _rope(c_k.reshape(nb_, t, N_ATTN_HEADS, HEAD_DIM), pos)
    q = q.reshape(nb_, t, N_ATTN_GROUPS, HEADS_PER_GROUP, HEAD_DIM)
    k = k.reshape(nb_, t, N_ATTN_GROUPS, HEADS_PER_GROUP, HEAD_DIM)
    v = c_v.reshape(nb_, t, N_ATTN_GROUPS, HEADS_PER_GROUP, HEAD_DIM)
    outs, lses, kv_new = [], [], []
    for g in range(N_ATTN_GROUPS):
        o_g, lse_g, kv_g = attend(g, q[:, :, g], k[:, :, g], v[:, :, g])
        outs.append(o_g)
        lses.append(lse_g)
        kv_new.append(kv_g)
    wts = jax.nn.softmax(jnp.stack(lses, axis=2), axis=2)
    o = jnp.einsum('btgh,btghd->bthd', wts.astype(x.dtype), jnp.stack(outs, axis=2))
    y_c = o.reshape(nb_, t, W_C) * jax.nn.silu(c_z)

    vn = _layernorm(d_v, ln_d_g, ln_d_b)
    y_d = d_u * _chunk_mix(vn, w_s, b_s) * jax.nn.silu(d_z)

    gate = jax.nn.sigmoid(g_log + b_gate.astype(x.dtype)).reshape(nb_, t, N_BRANCH, D_MODEL)
    merged = (gate[:, :, 0] * (y_a @ w_out_a) + gate[:, :, 1] * (y_b @ w_out_b)
              + gate[:, :, 2] * (y_c @ w_out_c) + gate[:, :, 3] * (y_d @ w_out_d))
    y = _layernorm(ALPHA * x + merged @ w_o, ln_g, ln_b)
    return y, ua[:, -(CONV_A - 1):], ub[:, -(CONV_B - 1):], kv_new, vn


def setup_inputs(seed: int = 0) -> dict:
    key = jax.random.key(seed)
    ks = jax.random.split(key, 26)

    def nrm(k, shape, scale):
        return jax.random.normal(k, shape, jnp.float32) * scale

    lens = [min(w, PAST_LEN) for w in WINDOWS]
    kvs = (HEADS_PER_GROUP, HEAD_DIM)
    return {
        "x_prompt": nrm(ks[0], (BATCH, SEQ, D_MODEL), 1.0),
        "x_sample": nrm(ks[1], (DEC_BATCH, DEC_SEQ, D_MODEL), 1.0),
        "state_conv_a": nrm(ks[2], (DEPTH, DEC_BATCH, CONV_A - 1, W_A), 1.0),
        "state_conv_b": nrm(ks[3], (DEPTH, DEC_BATCH, CONV_B - 1, W_B), 1.0),
        "cache_kv_w128": nrm(ks[4], (DEPTH, DEC_BATCH, lens[0], 2) + kvs, 1.0),
        "cache_kv_w512": nrm(ks[5], (DEPTH, DEC_BATCH, lens[1], 2) + kvs, 1.0),
        "cache_kv_w2048": nrm(ks[6], (DEPTH, DEC_BATCH, lens[2], 2) + kvs, 1.0),
        "w_in": nrm(ks[7], (DEPTH, D_MODEL, N_IN), D_MODEL ** -0.5),
        "conv_a_w": nrm(ks[8], (DEPTH, CONV_A, W_A), CONV_A ** -0.5),
        "conv_b_w": nrm(ks[9], (DEPTH, CONV_B, W_B), CONV_B ** -0.5),
        "conv_b_bias": nrm(ks[10], (DEPTH, W_B), 0.01),
        "ln_b_g": 1.0 + nrm(ks[11], (DEPTH, W_B), 0.01),
        "ln_b_b": nrm(ks[12], (DEPTH, W_B), 0.01),
        "ln_d_g": 1.0 + nrm(ks[13], (DEPTH, W_D), 0.01),
        "ln_d_b": nrm(ks[14], (DEPTH, W_D), 0.01),
        "w_s": nrm(ks[15], (DEPTH, D_GROUPS, CHUNK, CHUNK), CHUNK ** -0.5),
        "b_s": 1.0 + nrm(ks[16], (DEPTH, D_GROUPS, CHUNK), 0.1),
        "w_out_a": nrm(ks[17], (DEPTH, W_A, D_MODEL), BETA * W_A ** -0.5),
        "w_out_b": nrm(ks[18], (DEPTH, W_B, D_MODEL), BETA * W_B ** -0.5),
        "w_out_c": nrm(ks[19], (DEPTH, W_C, D_MODEL), BETA * W_C ** -0.5),
        "w_out_d": nrm(ks[20], (DEPTH, W_D, D_MODEL), BETA * W_D ** -0.5),
        "b_gate": nrm(ks[21], (DEPTH, N_BRANCH * D_MODEL), 0.01),
        "w_o": nrm(ks[22], (DEPTH, D_MODEL, D_MODEL), BETA * D_MODEL ** -0.5),
        "ln_g": 1.0 + nrm(ks[23], (DEPTH, D_MODEL), 0.01),
        "ln_b": nrm(ks[24], (DEPTH, D_MODEL), 0.01),
    }


def reference(x_prompt, x_sample, state_conv_a, state_conv_b, cache_kv_w128, cache_kv_w512,
              cache_kv_w2048, w_in, conv_a_w, conv_b_w, conv_b_bias, ln_b_g, ln_b_b, ln_d_g,
              ln_d_b, w_s, b_s, w_out_a, w_out_b, w_out_c, w_out_d, b_gate, w_o, ln_g, ln_b):
    params = (w_in, conv_a_w, conv_b_w, conv_b_bias, ln_b_g, ln_b_b, ln_d_g, ln_d_b, w_s, b_s,
              w_out_a, w_out_b, w_out_c, w_out_d, b_gate, w_o, ln_g, ln_b)
    caches = (cache_kv_w128, cache_kv_w512, cache_kv_w2048)
    xp, xs = x_prompt, x_sample
    bp, tp, _ = xp.shape
    ts = xs.shape[1]
    pos_p = jnp.arange(tp, dtype=jnp.int32)
    pos_s = PAST_LEN + jnp.arange(ts, dtype=jnp.int32)

    ca_p_l, ca_s_l, cb_p_l, cb_s_l, dv_s_l = [], [], [], [], []
    kv_p_l = [[] for _ in range(N_ATTN_GROUPS)]
    kv_s_l = [[] for _ in range(N_ATTN_GROUPS)]
    for l in range(DEPTH):
        lw = tuple(p[l] for p in params)

        def attend_prompt(g, q, k, v):
            span = WINDOWS[g] // DILATIONS[g]
            o, lse = _dilated_prompt(q, k, v, DILATIONS[g], span)
            keep = min(WINDOWS[g], tp)
            return o, lse, jnp.stack([k, v], axis=2)[:, -keep:]

        def attend_sample(g, q, k, v, l=l):
            buf = caches[g][l].astype(k.dtype)
            n_hist = buf.shape[1]
            k_ext = jnp.concatenate([buf[:, :, 0], k], axis=1)
            v_ext = jnp.concatenate([buf[:, :, 1], v], axis=1)
            span = WINDOWS[g] // DILATIONS[g]
            o, lse = _dilated_decode(q, k_ext, v_ext, n_hist, DILATIONS[g], span)
            return o, lse, jnp.stack([k, v], axis=2)

        hist_a0 = jnp.zeros((bp, CONV_A - 1, W_A), xp.dtype)
        hist_b0 = jnp.zeros((bp, CONV_B - 1, W_B), xp.dtype)
        xp, ca_p, cb_p, kv_p, _ = _layer(xp, pos_p, hist_a0, hist_b0, attend_prompt, lw)
        xs, ca_s, cb_s, kv_s, vn_s = _layer(xs, pos_s, state_conv_a[l], state_conv_b[l],
                                            attend_sample, lw)
        ca_p_l.append(ca_p)
        ca_s_l.append(ca_s)
        cb_p_l.append(cb_p)
        cb_s_l.append(cb_s)
        dv_s_l.append(vn_s)
        for g in range(N_ATTN_GROUPS):
            kv_p_l[g].append(kv_p[g])
            kv_s_l[g].append(kv_s[g])

    new_conv_a_prompt = jnp.stack(ca_p_l)
    new_conv_a_sample = jnp.stack(ca_s_l)
    new_conv_b_prompt = jnp.stack(cb_p_l)
    new_conv_b_sample = jnp.stack(cb_s_l)
    new_kv_w128_prompt = jnp.stack(kv_p_l[0])
    new_kv_w128_sample = jnp.stack(kv_s_l[0])
    new_kv_w512_prompt = jnp.stack(kv_p_l[1])
    new_kv_w512_sample = jnp.stack(kv_s_l[1])
    new_kv_w2048_prompt = jnp.stack(kv_p_l[2])
    new_kv_w2048_sample = jnp.stack(kv_s_l[2])
    new_gmlp_v_sample = jnp.stack(dv_s_l)
    return (xp, xs, new_conv_a_prompt, new_conv_a_sample, new_conv_b_prompt, new_conv_b_sample,
            new_kv_w128_prompt, new_kv_w128_sample, new_kv_w512_prompt, new_kv_w512_sample,
            new_kv_w2048_prompt, new_kv_w2048_sample, new_gmlp_v_sample)
```

```python
import functools

import jax
import jax.numpy as jnp
from jax import lax
from jax.experimental import pallas as pl
from jax.experimental.pallas import tpu as pltpu

D_MODEL = 4096
DEPTH = 2
PAST_LEN = 16384
W_A = 1024
W_B = 1024
HEAD_DIM = 128
N_GROUPS = 3
HEADS_PER_GROUP = 4
N_HEADS = N_GROUPS * HEADS_PER_GROUP
W_QKV = N_HEADS * HEAD_DIM
W_C = HEADS_PER_GROUP * HEAD_DIM
W_D = 1024
CHUNK = 128
D_GROUPS = 8
CONV_A = 3
CONV_B = 31
WINDOWS = (128, 512, 2048)
DILATIONS = (1, 4, 16)
SPAN = 128
ROPE_THETA = 10000.0
N_BRANCH = 4
LN_EPS = 1e-5
ALPHA = (2 * DEPTH) ** 0.25

COL_A_B, COL_A_C, COL_A_H, COL_A_Z = 0, 1024, 2048, 3072
COL_B_A, COL_B_G, COL_B_Z = 4096, 5120, 6144
COL_Q, COL_K, COL_V, COL_C_Z = 7168, 8704, 10240, 11776
COL_D_U, COL_D_V, COL_D_Z = 12288, 13312, 14336
N_BRANCH_COLS = 15360
COL_GATE = 15360

SAMPLE_ROWS = 16
VMEM_LIMIT_BYTES = 56 * 1024 * 1024

F32 = jnp.float32
BF16 = jnp.bfloat16


def _params(*sem):
    return pltpu.CompilerParams(dimension_semantics=sem, vmem_limit_bytes=VMEM_LIMIT_BYTES)


def _silu(v):
    return v * jax.nn.sigmoid(v)


def _layernorm(v, g, b):
    mu = jnp.mean(v, axis=-1, keepdims=True)
    c = v - mu
    var = jnp.mean(c * c, axis=-1, keepdims=True)
    return c * lax.rsqrt(var + LN_EPS) * g + b


def _mm_kernel(x_ref, w_ref, o_ref):
    o_ref[...] = jnp.dot(x_ref[...], w_ref[...], preferred_element_type=F32)


def _in_proj(xb, w_in_b, l, tm, tn):
    m = xb.shape[0]
    return pl.pallas_call(
        _mm_kernel,
        grid=(m // tm, N_BRANCH_COLS // tn),
        in_specs=[pl.BlockSpec((tm, D_MODEL), lambda i, j: (i, 0)),
                  pl.BlockSpec((None, D_MODEL, tn), lambda i, j: (l, 0, j))],
        out_specs=pl.BlockSpec((tm, tn), lambda i, j: (i, j)),
        out_shape=jax.ShapeDtypeStruct((m, N_BRANCH_COLS), F32),
        compiler_params=_params("parallel", "parallel"),
        name="in_proj",
    )(xb, w_in_b)


A_PAD = 8


def _branch_a_kernel(ab_ref, ac_ref, ah_ref, az_ref, w_ref, y_ref, st_ref, ext_ref):
    t = pl.program_id(1)
    tt = ab_ref.shape[0]

    @pl.when(t == 0)
    def _():
        ext_ref[0:A_PAD, :] = jnp.zeros((A_PAD, W_A), F32)

    @pl.when(t > 0)
    def _():
        ext_ref[0:A_PAD, :] = ext_ref[tt:tt + A_PAD, :]

    u = ac_ref[...] * ah_ref[...]
    ext_ref[A_PAD:A_PAD + tt, :] = u
    w = w_ref[...]
    conv = (w[0:1, :] * ext_ref[A_PAD - 2:A_PAD - 2 + tt, :]
            + w[1:2, :] * ext_ref[A_PAD - 1:A_PAD - 1 + tt, :]
            + w[2:3, :] * u)
    y_ref[...] = (ab_ref[...] * conv * _silu(az_ref[...])).astype(BF16)
    st_ref[...] = ext_ref[A_PAD + tt - (CONV_A - 1):A_PAD + tt, :]


def _branch_a(h3, conv_a_w, l, tt):
    nb, s, _ = h3.shape
    col = lambda c: pl.BlockSpec((None, tt, W_A), lambda b, t: (b, t, c // W_A))
    return pl.pallas_call(
        _branch_a_kernel,
        grid=(nb, s // tt),
        in_specs=[col(COL_A_B), col(COL_A_C), col(COL_A_H), col(COL_A_Z),
                  pl.BlockSpec((None, CONV_A, W_A), lambda b, t: (l, 0, 0))],
        out_specs=[pl.BlockSpec((None, tt, W_A), lambda b, t: (b, t, 0)),
                   pl.BlockSpec((None, CONV_A - 1, W_A), lambda b, t: (b, 0, 0))],
        out_shape=[jax.ShapeDtypeStruct((nb, s, W_A), BF16),
                   jax.ShapeDtypeStruct((nb, CONV_A - 1, W_A), F32)],
        scratch_shapes=[pltpu.VMEM((A_PAD + tt, W_A), F32)],
        compiler_params=_params("parallel", "arbitrary"),
        name="branch_a",
    )(h3, h3, h3, h3, conv_a_w)


B_PAD = 32
B_ROWS = 64
LANES = 128


def _branch_b_kernel(ba_ref, bg_ref, bz_ref, w_ref, bias_ref, g_ref, b_ref,
                     y_ref, st_ref, ext_ref, cb_ref):
    t = pl.program_id(1)
    tt = ba_ref.shape[0]

    @pl.when(t == 0)
    def _():
        ext_ref[0:B_PAD, :] = jnp.zeros((B_PAD, W_B), F32)

    @pl.when(t > 0)
    def _():
        ext_ref[0:B_PAD, :] = ext_ref[tt:tt + B_PAD, :]

    ext_ref[B_PAD:B_PAD + tt, :] = ba_ref[...] * jax.nn.sigmoid(bg_ref[...])
    first = B_PAD - (CONV_B - 1)
    for c in range(W_B // LANES):
        cs = slice(c * LANES, (c + 1) * LANES)
        wc = w_ref[:, cs]
        for rc in range(tt // B_ROWS):
            acc = jnp.zeros((B_ROWS, LANES), F32)
            for k in range(CONV_B):
                r0 = first + k + rc * B_ROWS
                acc = acc + wc[k:k + 1, :] * ext_ref[r0:r0 + B_ROWS, cs]
            cb_ref[rc * B_ROWS:(rc + 1) * B_ROWS, cs] = acc
    cb = cb_ref[...] + bias_ref[...]
    ln = _layernorm(cb, g_ref[...], b_ref[...])
    y_ref[...] = (_silu(ln) * _silu(bz_ref[...])).astype(BF16)
    st_ref[...] = ext_ref[B_PAD + tt - (CONV_B - 1):B_PAD + tt, :]


def _row_param(p):
    return p.reshape(DEPTH, 1, p.shape[-1])


def _row_spec(l, c, nargs):
    if nargs == 1:
        return pl.BlockSpec((None, 1, c), lambda b: (l, 0, 0))
    return pl.BlockSpec((None, 1, c), lambda b, t: (l, 0, 0))


def _branch_b(h3, conv_b_w, conv_b_bias, ln_b_g, ln_b_b, l, tt):
    nb, s, _ = h3.shape
    col = lambda c: pl.BlockSpec((None, tt, W_B), lambda b, t: (b, t, c // W_B))
    return pl.pallas_call(
        _branch_b_kernel,
        grid=(nb, s // tt),
        in_specs=[col(COL_B_A), col(COL_B_G), col(COL_B_Z),
                  pl.BlockSpec((None, CONV_B, W_B), lambda b, t: (l, 0, 0)),
                  _row_spec(l, W_B, 2), _row_spec(l, W_B, 2), _row_spec(l, W_B, 2)],
        out_specs=[pl.BlockSpec((None, tt, W_B), lambda b, t: (b, t, 0)),
                   pl.BlockSpec((None, CONV_B - 1, W_B), lambda b, t: (b, 0, 0))],
        out_shape=[jax.ShapeDtypeStruct((nb, s, W_B), BF16),
                   jax.ShapeDtypeStruct((nb, CONV_B - 1, W_B), F32)],
        scratch_shapes=[pltpu.VMEM((B_PAD + tt, W_B), F32), pltpu.VMEM((tt, W_B), F32)],
        compiler_params=_params("parallel", "arbitrary"),
        name="branch_b",
    )(h3, h3, h3, conv_b_w, _row_param(conv_b_bias), _row_param(ln_b_g), _row_param(ln_b_b))


def _branch_d_kernel(du_ref, dv_ref, dz_ref, g_ref, b_ref, ws_ref, bs_ref, y_ref):
    tt = du_ref.shape[0]
    vb = _layernorm(dv_ref[...], g_ref[...], b_ref[...]).astype(BF16)
    row = lax.broadcasted_iota(jnp.int32, (CHUNK, CHUNK), 0)
    col = lax.broadcasted_iota(jnp.int32, (CHUNK, CHUNK), 1)
    causal = col <= row
    for g in range(D_GROUPS):
        cs = slice(g * LANES, (g + 1) * LANES)
        wm = jnp.where(causal, ws_ref[g], 0.0).astype(BF16)
        for c in range(tt // CHUNK):
            rs = slice(c * CHUNK, (c + 1) * CHUNK)
            mix = jnp.dot(wm, vb[rs, cs], preferred_element_type=F32) + bs_ref[:, cs]
            y_ref[rs, cs] = (du_ref[rs, cs] * mix * _silu(dz_ref[rs, cs])).astype(BF16)


def _bias_slab(b_s_l):
    return jnp.repeat(b_s_l.T, LANES, axis=1)


def _branch_d(h3, ln_d_g, ln_d_b, w_s, b_s, l, tt):
    nb, s, _ = h3.shape
    col = lambda c: pl.BlockSpec((None, tt, W_D), lambda b, t: (b, t, c // W_D))
    return pl.pallas_call(
        _branch_d_kernel,
        grid=(nb, s // tt),
        in_specs=[col(COL_D_U), col(COL_D_V), col(COL_D_Z),
                  _row_spec(l, W_D, 2), _row_spec(l, W_D, 2),
                  pl.BlockSpec((None, D_GROUPS, CHUNK, CHUNK), lambda b, t: (l, 0, 0, 0)),
                  pl.BlockSpec((CHUNK, W_D), lambda b, t: (0, 0))],
        out_specs=pl.BlockSpec((None, tt, W_D), lambda b, t: (b, t, 0)),
        out_shape=jax.ShapeDtypeStruct((nb, s, W_D), BF16),
        compiler_params=_params("parallel", "parallel"),
        name="branch_d",
    )(h3, h3, h3, _row_param(ln_d_g), _row_param(ln_d_b), w_s, _bias_slab(b_s[l]))


def _rope_tables(pos):
    half = HEAD_DIM // 2
    inv = ROPE_THETA ** (-jnp.arange(half, dtype=F32) / half)
    ang = pos.astype(F32)[:, None] * inv[None, :]
    cos = jnp.cos(ang)
    sin = jnp.sin(ang)
    return jnp.concatenate([cos, cos], axis=-1), jnp.concatenate([-sin, sin], axis=-1)


def _rope(v, cos, sin):
    return v * cos + pltpu.roll(v, HEAD_DIM // 2, 1) * sin


def _rope_kernel(x_ref, cos_ref, sin_ref, o_ref):
    cos = cos_ref[...]
    sin = sin_ref[...]
    for hh in range(x_ref.shape[1] // HEAD_DIM):
        cs = slice(hh * HEAD_DIM, (hh + 1) * HEAD_DIM)
        o_ref[:, cs] = _rope(x_ref[:, cs], cos, sin)


def _rope_qk(h3, cos, sin, tt):
    nb, s, _ = h3.shape
    wblk = 512
    return pl.pallas_call(
        _rope_kernel,
        grid=(nb, s // tt, 2 * W_QKV // wblk),
        in_specs=[pl.BlockSpec((None, tt, wblk), lambda b, t, c: (b, t, COL_Q // wblk + c)),
                  pl.BlockSpec((tt, HEAD_DIM), lambda b, t, c: (t, 0)),
                  pl.BlockSpec((tt, HEAD_DIM), lambda b, t, c: (t, 0))],
        out_specs=pl.BlockSpec((None, tt, wblk), lambda b, t, c: (b, t, c)),
        out_shape=jax.ShapeDtypeStruct((nb, s, 2 * W_QKV), F32),
        compiler_params=_params("parallel", "parallel", "parallel"),
        name="rope_qk",
    )(h3, cos, sin)


def _attn_kernel(q_ref, kc_ref, kp_ref, vc_ref, vp_ref, o_ref, l_ref, *, dil, blocks, use_prev):
    n = pl.program_id(1)
    scale = HEAD_DIM ** -0.5
    row = lax.broadcasted_iota(jnp.int32, (SPAN, SPAN), 0)
    col = lax.broadcasted_iota(jnp.int32, (SPAN, SPAN), 1)
    own_mask = col <= row
    band_mask = col >= row
    first_mask = jnp.logical_and(band_mask, jnp.zeros((SPAN, SPAN), jnp.int32) + n > 0)
    nt = (((1,), (1,)), ((), ()))

    def rows(j, r):
        if dil == 1:
            return pl.ds(j * SPAN, SPAN)
        return pl.ds(j * SPAN * dil + r, SPAN, stride=dil)

    for r in range(dil):
        for j in range(blocks):
            idx = rows(j, r)
            q = q_ref[idx, :].astype(BF16)
            k = kc_ref[idx, :].astype(BF16)
            v = vc_ref[idx, :].astype(BF16)
            s_own = lax.dot_general(q, k, nt, preferred_element_type=F32) * scale
            s_own = jnp.where(own_mask, s_own, -jnp.inf)
            mx = jnp.max(s_own, axis=1, keepdims=True)
            if use_prev:
                if j > 0:
                    kp = kc_ref[rows(j - 1, r), :].astype(BF16)
                    vp = vc_ref[rows(j - 1, r), :].astype(BF16)
                    pmask = band_mask
                else:
                    kp = kp_ref[rows(0, r), :].astype(BF16)
                    vp = vp_ref[rows(0, r), :].astype(BF16)
                    pmask = first_mask
                s_prev = lax.dot_general(q, kp, nt, preferred_element_type=F32) * scale
                s_prev = jnp.where(pmask, s_prev, -jnp.inf)
                mx = jnp.maximum(mx, jnp.max(s_prev, axis=1, keepdims=True))
            p_own = jnp.exp(s_own - mx)
            den = jnp.sum(p_own, axis=1, keepdims=True)
            if use_prev:
                p_prev = jnp.exp(s_prev - mx)
                den = den + jnp.sum(p_prev, axis=1, keepdims=True)
            o = jnp.dot((p_own / den).astype(BF16), v, preferred_element_type=F32)
            if use_prev:
                o = o + jnp.dot((p_prev / den).astype(BF16), vp, preferred_element_type=F32)
            o_ref[idx, :] = o
            l_ref[idx, :] = jnp.broadcast_to(mx + jnp.log(den), (SPAN, HEAD_DIM))


def _attn_group(qk3, h3, g, blocks):
    nb, s, _ = h3.shape
    dil = DILATIONS[g]
    unit = SPAN * dil
    tile = unit * blocks
    use_prev = s > unit
    per = tile // unit
    cq = lambda hs: g * HEADS_PER_GROUP + hs
    ck = lambda hs: N_HEADS + g * HEADS_PER_GROUP + hs
    cv = lambda hs: COL_V // HEAD_DIM + g * HEADS_PER_GROUP + hs
    prev = lambda n: jnp.maximum(n * per - 1, 0)
    kern = functools.partial(_attn_kernel, dil=dil, blocks=blocks, use_prev=use_prev)
    return pl.pallas_call(
        kern,
        grid=(nb, s // tile, HEADS_PER_GROUP),
        in_specs=[pl.BlockSpec((None, tile, HEAD_DIM), lambda b, n, hs: (b, n, cq(hs))),
                  pl.BlockSpec((None, tile, HEAD_DIM), lambda b, n, hs: (b, n, ck(hs))),
                  pl.BlockSpec((None, unit, HEAD_DIM), lambda b, n, hs: (b, prev(n), ck(hs))),
                  pl.BlockSpec((None, tile, HEAD_DIM), lambda b, n, hs: (b, n, cv(hs))),
                  pl.BlockSpec((None, unit, HEAD_DIM), lambda b, n, hs: (b, prev(n), cv(hs)))],
        out_specs=[pl.BlockSpec((None, tile, HEAD_DIM), lambda b, n, hs: (b, n, hs)),
                   pl.BlockSpec((None, tile, HEAD_DIM), lambda b, n, hs: (b, n, hs))],
        out_shape=[jax.ShapeDtypeStruct((nb, s, W_C), F32),
                   jax.ShapeDtypeStruct((nb, s, W_C), F32)],
        compiler_params=_params("parallel", "parallel", "parallel"),
        name=f"attn_g{g}",
    )(qk3, qk3, qk3, h3, h3)


def _merge_groups(os_, ls_):
    mx = jnp.maximum(jnp.maximum(ls_[0], ls_[1]), ls_[2])
    es = [jnp.exp(l_ - mx) for l_ in ls_]
    den = es[0] + es[1] + es[2]
    return (es[0] / den) * os_[0] + (es[1] / den) * os_[1] + (es[2] / den) * os_[2]


def _combine_kernel(o0, o1, o2, l0, l1, l2, cz_ref, y_ref):
    o = _merge_groups([o0[...], o1[...], o2[...]], [l0[...], l1[...], l2[...]])
    y_ref[...] = (o * _silu(cz_ref[...])).astype(BF16)


def _combine(os_, ls_, h3, tt):
    nb, s, _ = h3.shape
    spec = pl.BlockSpec((None, tt, W_C), lambda b, t: (b, t, 0))
    return pl.pallas_call(
        _combine_kernel,
        grid=(nb, s // tt),
        in_specs=[spec] * 6 + [pl.BlockSpec((None, tt, W_C), lambda b, t: (b, t, COL_C_Z // W_C))],
        out_specs=spec,
        out_shape=jax.ShapeDtypeStruct((nb, s, W_C), BF16),
        compiler_params=_params("parallel", "parallel"),
        name="attn_combine",
    )(*os_, *ls_, h3)


def _gate_out_kernel(x_ref, wg0, wg1, wg2, wg3, bg0, bg1, bg2, bg3,
                     ya, yb, yc, yd, woa, wob, woc, wod, o_ref):
    x = x_ref[...]
    acc = None
    for wg, bg, y, wo in ((wg0, bg0, ya, woa), (wg1, bg1, yb, wob),
                          (wg2, bg2, yc, woc), (wg3, bg3, yd, wod)):
        gate = jax.nn.sigmoid(jnp.dot(x, wg[...], preferred_element_type=F32) + bg[...])
        part = jnp.dot(y[...], wo[...], preferred_element_type=F32)
        acc = gate * part if acc is None else acc + gate * part
    o_ref[...] = acc.astype(BF16)


def _gate_out(xb, w_in_b, b_gate, ys, wouts_b, l, tm, tn):
    m = xb.shape[0]
    nj = D_MODEL // tn
    wg_spec = lambda br: pl.BlockSpec(
        (None, D_MODEL, tn), lambda i, j: (l, 0, (COL_GATE + br * D_MODEL) // tn + j))
    bg_spec = lambda br: pl.BlockSpec((None, 1, tn), lambda i, j: (l, 0, br * nj + j))
    y_spec = lambda y: pl.BlockSpec((tm, y.shape[1]), lambda i, j: (i, 0))
    wo_spec = lambda w: pl.BlockSpec((None, w.shape[1], tn), lambda i, j: (l, 0, j))
    bg3 = b_gate.reshape(DEPTH, 1, N_BRANCH * D_MODEL)
    return pl.pallas_call(
        _gate_out_kernel,
        grid=(m // tm, nj),
        in_specs=([pl.BlockSpec((tm, D_MODEL), lambda i, j: (i, 0))]
                  + [wg_spec(br) for br in range(N_BRANCH)]
                  + [bg_spec(br) for br in range(N_BRANCH)]
                  + [y_spec(y) for y in ys]
                  + [wo_spec(w) for w in wouts_b]),
        out_specs=pl.BlockSpec((tm, tn), lambda i, j: (i, j)),
        out_shape=jax.ShapeDtypeStruct((m, D_MODEL), BF16),
        compiler_params=_params("parallel", "parallel"),
        name="gate_out",
    )(xb, *([w_in_b] * N_BRANCH), *([bg3] * N_BRANCH), *ys, *wouts_b)


def _out_ln_kernel(m_ref, wo_ref, x_ref, g_ref, b_ref, y_ref, yb_ref, z_ref):
    j = pl.program_id(1)
    nj, _, tn = z_ref.shape
    z_ref[j] = ALPHA * x_ref[...] + jnp.dot(m_ref[...], wo_ref[...], preferred_element_type=F32)

    @pl.when(j == nj - 1)
    def _():
        tot = z_ref[0].sum(axis=-1, keepdims=True)
        for c in range(1, nj):
            tot = tot + z_ref[c].sum(axis=-1, keepdims=True)
        mu = tot / D_MODEL
        sq = None
        for c in range(nj):
            d = z_ref[c] - mu
            part = (d * d).sum(axis=-1, keepdims=True)
            sq = part if sq is None else sq + part
        rstd = lax.rsqrt(sq / D_MODEL + LN_EPS)
        for c in range(nj):
            cs = slice(c * tn, (c + 1) * tn)
            y = (z_ref[c] - mu) * rstd * g_ref[:, cs] + b_ref[:, cs]
            y_ref[:, cs] = y
            yb_ref[:, cs] = y.astype(BF16)


def _out_ln(merged, w_o_b, x, ln_g, ln_b, l, tm, tn):
    m = x.shape[0]
    return pl.pallas_call(
        _out_ln_kernel,
        grid=(m // tm, D_MODEL // tn),
        in_specs=[pl.BlockSpec((tm, D_MODEL), lambda i, j: (i, 0)),
                  pl.BlockSpec((None, D_MODEL, tn), lambda i, j: (l, 0, j)),
                  pl.BlockSpec((tm, tn), lambda i, j: (i, j)),
                  pl.BlockSpec((None, 1, D_MODEL), lambda i, j: (l, 0, 0)),
                  pl.BlockSpec((None, 1, D_MODEL), lambda i, j: (l, 0, 0))],
        out_specs=[pl.BlockSpec((tm, D_MODEL), lambda i, j: (i, 0)),
                   pl.BlockSpec((tm, D_MODEL), lambda i, j: (i, 0))],
        out_shape=[jax.ShapeDtypeStruct((m, D_MODEL), F32),
                   jax.ShapeDtypeStruct((m, D_MODEL), BF16)],
        scratch_shapes=[pltpu.VMEM((D_MODEL // tn, tm, tn), F32)],
        compiler_params=_params("parallel", "arbitrary"),
        name="out_ln",
    )(merged, w_o_b, x, _row_param(ln_g), _row_param(ln_b))


def _sample_mix_kernel(h_ref, ha_ref, hb_ref, wa_ref, wb_ref, bias_ref, lbg_ref, lbb_ref,
                       ldg_ref, ldb_ref, ws0_ref, bs0_ref, cos_ref, sin_ref,
                       ya_ref, yb_ref, yd_ref, na_ref, nb_ref, vn_ref, qk_ref):
    col = lambda c, w: h_ref[:, c:c + w]
    u_a = col(COL_A_C, W_A) * col(COL_A_H, W_A)
    conv_a = wa_ref[0:1, :] * ha_ref[0] + wa_ref[1:2, :] * ha_ref[1] + wa_ref[2:3, :] * u_a
    ya_ref[...] = col(COL_A_B, W_A) * conv_a * _silu(col(COL_A_Z, W_A))
    na_ref[0] = ha_ref[1]
    na_ref[1] = u_a
    u_b = col(COL_B_A, W_B) * jax.nn.sigmoid(col(COL_B_G, W_B))
    cb = wb_ref[CONV_B - 1:CONV_B, :] * u_b + bias_ref[...]
    for k in range(CONV_B - 1):
        cb = cb + wb_ref[k:k + 1, :] * hb_ref[k]
    yb_ref[...] = _silu(_layernorm(cb, lbg_ref[...], lbb_ref[...])) * _silu(col(COL_B_Z, W_B))
    for k in range(CONV_B - 2):
        nb_ref[k] = hb_ref[k + 1]
    nb_ref[CONV_B - 2] = u_b
    vn = _layernorm(col(COL_D_V, W_D), ldg_ref[...], ldb_ref[...])
    vn_ref[...] = vn
    mix = ws0_ref[...] * vn + bs0_ref[...]
    yd_ref[...] = col(COL_D_U, W_D) * mix * _silu(col(COL_D_Z, W_D))
    cos = cos_ref[...]
    sin = sin_ref[...]
    for hh in range(2 * N_HEADS):
        cs = slice(hh * HEAD_DIM, (hh + 1) * HEAD_DIM)
        qk_ref[:, cs] = _rope(h_ref[:, COL_Q + hh * HEAD_DIM:COL_Q + (hh + 1) * HEAD_DIM], cos, sin)


def _sample_mix(hs, hist_a, hist_b, conv_a_w, conv_b_w, conv_b_bias, ln_b_g, ln_b_b,
                ln_d_g, ln_d_b, w_s, b_s, cos, sin, l):
    r = SAMPLE_ROWS
    ws0 = jnp.repeat(w_s[l, :, 0, 0], LANES)[None, :]
    bs0 = jnp.repeat(b_s[l, :, 0], LANES)[None, :]
    row = lambda p: p[l][None, :]
    f = lambda shape: jax.ShapeDtypeStruct(shape, F32)
    return pl.pallas_call(
        _sample_mix_kernel,
        out_shape=[f((r, W_A)), f((r, W_B)), f((r, W_D)), f((CONV_A - 1, r, W_A)),
                   f((CONV_B - 1, r, W_B)), f((r, W_D)), f((r, 2 * W_QKV))],
        compiler_params=pltpu.CompilerParams(vmem_limit_bytes=VMEM_LIMIT_BYTES),
        name="sample_mix",
    )(hs, hist_a, hist_b, conv_a_w[l], conv_b_w[l], row(conv_b_bias), row(ln_b_g), row(ln_b_b),
      row(ln_d_g), row(ln_d_b), ws0, bs0, cos, sin)


def _sample_attn_kernel(qk_ref, h_ref, *refs):
    y_ref = refs[-1]
    per_group = 2 * HEADS_PER_GROUP
    b = pl.program_id(0)
    scale = HEAD_DIM ** -0.5

    @pl.when(b == 0)
    def _():
        y_ref[...] = jnp.zeros(y_ref.shape, F32)

    is_b = lax.broadcasted_iota(jnp.int32, (y_ref.shape[0], HEAD_DIM), 0) == b

    def pick(ref, c0):
        return jnp.sum(jnp.where(is_b, ref[:, c0:c0 + HEAD_DIM], 0.0), axis=0, keepdims=True)

    for hs in range(HEADS_PER_GROUP):
        os_, ls_ = [], []
        for g in range(N_GROUPS):
            dil = DILATIONS[g]
            hcol = (g * HEADS_PER_GROUP + hs) * HEAD_DIM
            q = pick(qk_ref, hcol)
            k_new = pick(qk_ref, W_QKV + hcol)
            v_new = pick(h_ref, COL_V + hcol)
            kc_ref = refs[g * per_group + hs]
            vc_ref = refs[g * per_group + HEADS_PER_GROUP + hs]
            start = kc_ref.shape[0] - SPAN * dil
            ridx = pl.ds(start, SPAN) if dil == 1 else pl.ds(start, SPAN, stride=dil)
            kc = kc_ref[ridx, :]
            vc = vc_ref[ridx, :]
            s_c = jnp.sum(kc * q, axis=1, keepdims=True) * scale
            s_n = jnp.sum(k_new * q, axis=1, keepdims=True) * scale
            mx = jnp.maximum(jnp.max(s_c, axis=0, keepdims=True), s_n)
            p_c = jnp.exp(s_c - mx)
            p_n = jnp.exp(s_n - mx)
            den = jnp.sum(p_c, axis=0, keepdims=True) + p_n
            o = jnp.sum((p_c / den) * vc, axis=0, keepdims=True) + (p_n / den) * v_new
            os_.append(o)
            ls_.append(mx + jnp.log(den))
        o = _merge_groups(os_, ls_)
        y = o * _silu(pick(h_ref, COL_C_Z + hs * HEAD_DIM))
        cs = slice(hs * HEAD_DIM, (hs + 1) * HEAD_DIM)
        y_ref[:, cs] = jnp.where(is_b, jnp.broadcast_to(y, is_b.shape), y_ref[:, cs])


def _sample_attn(qk, hs, caches, l, nb):
    r = SAMPLE_ROWS
    cspec = lambda c, j: pl.BlockSpec((None, None, c.shape[2], HEAD_DIM), lambda b: (l, b, 0, j))
    cache_specs, cache_args = [], []
    for c in caches:
        for j in range(2 * HEADS_PER_GROUP):
            cache_specs.append(cspec(c, j))
            cache_args.append(c)
    return pl.pallas_call(
        _sample_attn_kernel,
        grid=(nb,),
        in_specs=[pl.BlockSpec((r, 2 * W_QKV), lambda b: (0, 0)),
                  pl.BlockSpec((r, N_BRANCH_COLS), lambda b: (0, 0))] + cache_specs,
        out_specs=pl.BlockSpec((r, W_C), lambda b: (0, 0)),
        out_shape=jax.ShapeDtypeStruct((r, W_C), F32),
        compiler_params=_params("arbitrary"),
        name="sample_attn",
    )(qk, hs, *cache_args)


def _kv_rows(qk3, h3, g, keep):
    nb = qk3.shape[0]
    c0 = g * W_C
    k = qk3[:, -keep:, W_QKV + c0:W_QKV + c0 + W_C]
    v = h3[:, -keep:, COL_V + c0:COL_V + c0 + W_C]
    return jnp.stack([k, v], axis=2).reshape(nb, keep, 2, HEADS_PER_GROUP, HEAD_DIM)


def kernel(x_prompt, x_sample, state_conv_a, state_conv_b, cache_kv_w128, cache_kv_w512,
           cache_kv_w2048, w_in, conv_a_w, conv_b_w, conv_b_bias, ln_b_g, ln_b_b, ln_d_g,
           ln_d_b, w_s, b_s, w_out_a, w_out_b, w_out_c, w_out_d, b_gate, w_o, ln_g, ln_b):
    bp, tp, _ = x_prompt.shape
    bs, ts, _ = x_sample.shape
    assert ts == 1 and bs <= SAMPLE_ROWS
    mp = bp * tp
    pad = SAMPLE_ROWS - bs

    w_in_b = w_in.astype(BF16)
    w_o_b = w_o.astype(BF16)
    wouts_b = [w.astype(BF16) for w in (w_out_a, w_out_b, w_out_c, w_out_d)]
    caches = [c.reshape(c.shape[0], c.shape[1], c.shape[2], 2 * W_C)
              for c in (cache_kv_w128, cache_kv_w512, cache_kv_w2048)]

    cos_p, sin_p = _rope_tables(jnp.arange(tp, dtype=jnp.int32))
    cos_s, sin_s = _rope_tables(PAST_LEN + jnp.arange(ts, dtype=jnp.int32))

    xp = x_prompt.reshape(mp, D_MODEL)
    xpb = xp.astype(BF16)
    xs = jnp.pad(x_sample.reshape(bs, D_MODEL), ((0, pad), (0, 0)))
    xsb = xs.astype(BF16)

    ca_p, ca_s, cb_p, cb_s, dv_s = [], [], [], [], []
    kv_p = [[] for _ in range(N_GROUPS)]
    kv_s = [[] for _ in range(N_GROUPS)]
    for l in range(DEPTH):
        h3 = _in_proj(xpb, w_in_b, l, 1024, 1024).reshape(bp, tp, N_BRANCH_COLS)
        y_a, st_a = _branch_a(h3, conv_a_w, l, 512)
        y_b, st_b = _branch_b(h3, conv_b_w, conv_b_bias, ln_b_g, ln_b_b, l, 256)
        y_d = _branch_d(h3, ln_d_g, ln_d_b, w_s, b_s, l, 512)
        qk3 = _rope_qk(h3, cos_p, sin_p, 512)
        outs = [_attn_group(qk3, h3, g, 4 if DILATIONS[g] == 1 else 1) for g in range(N_GROUPS)]
        y_c = _combine([o for o, _ in outs], [l_ for _, l_ in outs], h3, 512)
        ys = [y.reshape(mp, y.shape[-1]) for y in (y_a, y_b, y_c, y_d)]
        merged = _gate_out(xpb, w_in_b, b_gate, ys, wouts_b, l, 512, 256)
        xp, xpb = _out_ln(merged, w_o_b, xp, ln_g, ln_b, l, 512, 256)
        ca_p.append(st_a)
        cb_p.append(st_b)
        for g in range(N_GROUPS):
            kv_p[g].append(_kv_rows(qk3, h3, g, min(WINDOWS[g], tp)))

        hs = _in_proj(xsb, w_in_b, l, SAMPLE_ROWS, 1024)
        hist_a = jnp.pad(state_conv_a[l].transpose(1, 0, 2), ((0, 0), (0, pad), (0, 0)))
        hist_b = jnp.pad(state_conv_b[l].transpose(1, 0, 2), ((0, 0), (0, pad), (0, 0)))
        ya_s, yb_s, yd_s, na, nb_, vn, qk_s = _sample_mix(
            hs, hist_a, hist_b, conv_a_w, conv_b_w, conv_b_bias, ln_b_g, ln_b_b,
            ln_d_g, ln_d_b, w_s, b_s, cos_s, sin_s, l)
        yc_s = _sample_attn(qk_s, hs, caches, l, bs)
        ys_s = [y.astype(BF16) for y in (ya_s, yb_s, yc_s, yd_s)]
        merged_s = _gate_out(xsb, w_in_b, b_gate, ys_s, wouts_b, l, SAMPLE_ROWS, 256)
        xs, xsb = _out_ln(merged_s, w_o_b, xs, ln_g, ln_b, l, SAMPLE_ROWS, 512)
        ca_s.append(na[:, :bs].transpose(1, 0, 2))
        cb_s.append(nb_[:, :bs].transpose(1, 0, 2))
        dv_s.append(vn[:bs].reshape(bs, ts, W_D))
        for g in range(N_GROUPS):
            c0 = g * W_C
            k = qk_s[:bs, W_QKV + c0:W_QKV + c0 + W_C]
            v = hs[:bs, COL_V + c0:COL_V + c0 + W_C]
            kv_s[g].append(jnp.stack([k, v], axis=1).reshape(bs, ts, 2, HEADS_PER_GROUP, HEAD_DIM))

    return (xp.reshape(bp, tp, D_MODEL), xs[:bs].reshape(bs, ts, D_MODEL),
            jnp.stack(ca_p), jnp.stack(ca_s), jnp.stack(cb_p), jnp.stack(cb_s),
            jnp.stack(kv_p[0]), jnp.stack(kv_s[0]), jnp.stack(kv_p[1]), jnp.stack(kv_s[1]),
            jnp.stack(kv_p[2]), jnp.stack(kv_s[2]), jnp.stack(dv_s))
```

```python
import functools

import jax
import jax.numpy as jnp
from jax import lax
from jax.experimental import pallas as pl
from jax.experimental.pallas import tpu as pltpu

D_MODEL = 4096
DEPTH = 2
PAST_LEN = 16384
W_A = 1024
W_B = 1024
HEAD_DIM = 128
N_GROUPS = 3
HEADS_PER_GROUP = 4
N_HEADS = N_GROUPS * HEADS_PER_GROUP
W_QKV = N_HEADS * HEAD_DIM
W_C = HEADS_PER_GROUP * HEAD_DIM
W_D = 1024
CHUNK = 128
D_GROUPS = 8
CONV_A = 3
CONV_B = 31
WINDOWS = (128, 512, 2048)
DILATIONS = (1, 4, 16)
SPAN = 128
ROPE_THETA = 10000.0
N_BRANCH = 4
LN_EPS = 1e-5
ALPHA = (2 * DEPTH) ** 0.25

COL_A_B, COL_A_C, COL_A_H, COL_A_Z = 0, 1024, 2048, 3072
COL_B_A, COL_B_G, COL_B_Z = 4096, 5120, 6144
COL_Q, COL_K, COL_V, COL_C_Z = 7168, 8704, 10240, 11776
COL_D_U, COL_D_V, COL_D_Z = 12288, 13312, 14336
N_BRANCH_COLS = 15360
COL_GATE = 15360

SAMPLE_ROWS = 16
VMEM_LIMIT_BYTES = 56 * 1024 * 1024

F32 = jnp.float32
BF16 = jnp.bfloat16


def _params(*sem):
    return pltpu.CompilerParams(dimension_semantics=sem, vmem_limit_bytes=VMEM_LIMIT_BYTES)


def _silu(v):
    return v * jax.nn.sigmoid(v)


def _layernorm(v, g, b):
    mu = jnp.mean(v, axis=-1, keepdims=True)
    c = v - mu
    var = jnp.mean(c * c, axis=-1, keepdims=True)
    return c * lax.rsqrt(var + LN_EPS) * g + b


def _mm_kernel(x_ref, w_ref, o_ref):
    o_ref[...] = jnp.dot(x_ref[...], w_ref[...].astype(BF16), preferred_element_type=F32)


def _in_proj(xb, w_in, l, tm, tn):
    m = xb.shape[0]
    return pl.pallas_call(
        _mm_kernel,
        grid=(m // tm, N_BRANCH_COLS // tn),
        in_specs=[pl.BlockSpec((tm, D_MODEL), lambda i, j: (i, 0), pipeline_mode=pl.Buffered(1)),
                  pl.BlockSpec((None, D_MODEL, tn), lambda i, j: (l, 0, j))],
        out_specs=pl.BlockSpec((tm, tn), lambda i, j: (i, j)),
        out_shape=jax.ShapeDtypeStruct((m, N_BRANCH_COLS), F32),
        compiler_params=_params("parallel", "parallel"),
        name="in_proj",
    )(xb, w_in)


A_PAD = 8


def _branch_a_kernel(ab_ref, ac_ref, ah_ref, az_ref, w_ref, y_ref, st_ref, ext_ref):
    t = pl.program_id(1)
    tt = ab_ref.shape[0]

    @pl.when(t == 0)
    def _():
        ext_ref[0:A_PAD, :] = jnp.zeros((A_PAD, W_A), F32)

    @pl.when(t > 0)
    def _():
        ext_ref[0:A_PAD, :] = ext_ref[tt:tt + A_PAD, :]

    u = ac_ref[...] * ah_ref[...]
    ext_ref[A_PAD:A_PAD + tt, :] = u
    w = w_ref[...]
    conv = (w[0:1, :] * ext_ref[A_PAD - 2:A_PAD - 2 + tt, :]
            + w[1:2, :] * ext_ref[A_PAD - 1:A_PAD - 1 + tt, :]
            + w[2:3, :] * u)
    y_ref[...] = (ab_ref[...] * conv * _silu(az_ref[...])).astype(BF16)
    st_ref[...] = ext_ref[A_PAD + tt - (CONV_A - 1):A_PAD + tt, :]


def _branch_a(h3, conv_a_w, l, tt):
    nb, s, _ = h3.shape
    col = lambda c: pl.BlockSpec((None, tt, W_A), lambda b, t: (b, t, c // W_A))
    return pl.pallas_call(
        _branch_a_kernel,
        grid=(nb, s // tt),
        in_specs=[col(COL_A_B), col(COL_A_C), col(COL_A_H), col(COL_A_Z),
                  pl.BlockSpec((None, CONV_A, W_A), lambda b, t: (l, 0, 0))],
        out_specs=[pl.BlockSpec((None, tt, W_A), lambda b, t: (b, t, 0)),
                   pl.BlockSpec((None, CONV_A - 1, W_A), lambda b, t: (b, 0, 0))],
        out_shape=[jax.ShapeDtypeStruct((nb, s, W_A), BF16),
                   jax.ShapeDtypeStruct((nb, CONV_A - 1, W_A), F32)],
        scratch_shapes=[pltpu.VMEM((A_PAD + tt, W_A), F32)],
        compiler_params=_params("parallel", "arbitrary"),
        name="branch_a",
    )(h3, h3, h3, h3, conv_a_w)


B_PAD = 32
B_ROWS = 64
LANES = 128
SUBLANES = 8


def _branch_b_kernel(ba_ref, bg_ref, bz_ref, w_ref, bias_ref, g_ref, b_ref,
                     y_ref, st_ref, ext_ref, sh_ref, cb_ref):
    t = pl.program_id(1)
    tt = ba_ref.shape[0]

    @pl.when(t == 0)
    def _():
        ext_ref[0:B_PAD, :] = jnp.zeros((B_PAD, W_B), F32)

    @pl.when(t > 0)
    def _():
        ext_ref[0:B_PAD, :] = ext_ref[tt:tt + B_PAD, :]

    ext_ref[B_PAD:B_PAD + tt, :] = ba_ref[...] * jax.nn.sigmoid(bg_ref[...])
    sh_rows = sh_ref.shape[1]
    for s in range(1, SUBLANES):
        sh_ref[s - 1] = ext_ref[s:s + sh_rows, :]
    first = B_PAD - (CONV_B - 1)
    for c in range(W_B // LANES):
        cs = slice(c * LANES, (c + 1) * LANES)
        wc = w_ref[:, cs]
        for rc in range(tt // B_ROWS):
            acc = jnp.zeros((B_ROWS, LANES), F32)
            for k in range(CONV_B):
                q, s = divmod(first + k, SUBLANES)
                r0 = SUBLANES * q + rc * B_ROWS
                if s == 0:
                    tap = ext_ref[r0:r0 + B_ROWS, cs]
                else:
                    tap = sh_ref[s - 1, r0:r0 + B_ROWS, cs]
                acc = acc + wc[k:k + 1, :] * tap
            cb_ref[rc * B_ROWS:(rc + 1) * B_ROWS, cs] = acc
    cb = cb_ref[...] + bias_ref[...]
    ln = _layernorm(cb, g_ref[...], b_ref[...])
    y_ref[...] = (_silu(ln) * _silu(bz_ref[...])).astype(BF16)
    st_ref[...] = ext_ref[B_PAD + tt - (CONV_B - 1):B_PAD + tt, :]


def _row_param(p):
    return p.reshape(DEPTH, 1, p.shape[-1])


def _row_spec(l, c, nargs):
    if nargs == 1:
        return pl.BlockSpec((None, 1, c), lambda b: (l, 0, 0))
    return pl.BlockSpec((None, 1, c), lambda b, t: (l, 0, 0))


def _branch_b(h3, conv_b_w, conv_b_bias, ln_b_g, ln_b_b, l, tt):
    nb, s, _ = h3.shape
    col = lambda c: pl.BlockSpec((None, tt, W_B), lambda b, t: (b, t, c // W_B))
    return pl.pallas_call(
        _branch_b_kernel,
        grid=(nb, s // tt),
        in_specs=[col(COL_B_A), col(COL_B_G), col(COL_B_Z),
                  pl.BlockSpec((None, CONV_B, W_B), lambda b, t: (l, 0, 0)),
                  _row_spec(l, W_B, 2), _row_spec(l, W_B, 2), _row_spec(l, W_B, 2)],
        out_specs=[pl.BlockSpec((None, tt, W_B), lambda b, t: (b, t, 0)),
                   pl.BlockSpec((None, CONV_B - 1, W_B), lambda b, t: (b, 0, 0))],
        out_shape=[jax.ShapeDtypeStruct((nb, s, W_B), BF16),
                   jax.ShapeDtypeStruct((nb, CONV_B - 1, W_B), F32)],
        scratch_shapes=[pltpu.VMEM((B_PAD + tt, W_B), F32),
                        pltpu.VMEM((SUBLANES - 1, B_PAD + tt - SUBLANES, W_B), F32),
                        pltpu.VMEM((tt, W_B), F32)],
        compiler_params=_params("parallel", "arbitrary"),
        name="branch_b",
    )(h3, h3, h3, conv_b_w, _row_param(conv_b_bias), _row_param(ln_b_g), _row_param(ln_b_b))


def _branch_d_kernel(du_ref, dv_ref, dz_ref, g_ref, b_ref, ws_ref, bs_ref, y_ref):
    tt = du_ref.shape[0]
    vb = _layernorm(dv_ref[...], g_ref[...], b_ref[...]).astype(BF16)
    row = lax.broadcasted_iota(jnp.int32, (CHUNK, CHUNK), 0)
    col = lax.broadcasted_iota(jnp.int32, (CHUNK, CHUNK), 1)
    causal = col <= row
    for g in range(D_GROUPS):
        cs = slice(g * LANES, (g + 1) * LANES)
        wm = jnp.where(causal, ws_ref[g], 0.0).astype(BF16)
        for c in range(tt // CHUNK):
            rs = slice(c * CHUNK, (c + 1) * CHUNK)
            mix = jnp.dot(wm, vb[rs, cs], preferred_element_type=F32) + bs_ref[:, cs]
            y_ref[rs, cs] = (du_ref[rs, cs] * mix * _silu(dz_ref[rs, cs])).astype(BF16)


def _bias_slab(b_s_l):
    return jnp.repeat(b_s_l.T, LANES, axis=1)


def _branch_d(h3, ln_d_g, ln_d_b, w_s, b_s, l, tt):
    nb, s, _ = h3.shape
    col = lambda c: pl.BlockSpec((None, tt, W_D), lambda b, t: (b, t, c // W_D))
    return pl.pallas_call(
        _branch_d_kernel,
        grid=(nb, s // tt),
        in_specs=[col(COL_D_U), col(COL_D_V), col(COL_D_Z),
                  _row_spec(l, W_D, 2), _row_spec(l, W_D, 2),
                  pl.BlockSpec((None, D_GROUPS, CHUNK, CHUNK), lambda b, t: (l, 0, 0, 0)),
                  pl.BlockSpec((CHUNK, W_D), lambda b, t: (0, 0))],
        out_specs=pl.BlockSpec((None, tt, W_D), lambda b, t: (b, t, 0)),
        out_shape=jax.ShapeDtypeStruct((nb, s, W_D), BF16),
        compiler_params=_params("parallel", "parallel"),
        name="branch_d",
    )(h3, h3, h3, _row_param(ln_d_g), _row_param(ln_d_b), w_s, _bias_slab(b_s[l]))


def _rope_tables(pos):
    half = HEAD_DIM // 2
    inv = ROPE_THETA ** (-jnp.arange(half, dtype=F32) / half)
    ang = pos.astype(F32)[:, None] * inv[None, :]
    cos = jnp.cos(ang)
    sin = jnp.sin(ang)
    return jnp.concatenate([cos, cos], axis=-1), jnp.concatenate([-sin, sin], axis=-1)


def _rope(v, cos, sin):
    return v * cos + pltpu.roll(v, HEAD_DIM // 2, 1) * sin


def _merge_groups(os_, ls_):
    mx = jnp.maximum(jnp.maximum(ls_[0], ls_[1]), ls_[2])
    es = [jnp.exp(l_ - mx) for l_ in ls_]
    den = es[0] + es[1] + es[2]
    return (es[0] / den) * os_[0] + (es[1] / den) * os_[1] + (es[2] / den) * os_[2]


ATTN_ILP = 2


def _attn_kernel(*refs, dil, blocks, use_prev, keep, merge):
    refs = list(refs)
    q_ref, kc_ref, vc_ref, cos_ref, sin_ref = refs[:5]
    del refs[:5]
    if use_prev:
        kp_ref, vp_ref, cosp_ref, sinp_ref = refs[:4]
        del refs[:4]
    if merge:
        oa_ref, la_ref, ob_ref, lb_ref, cz_ref, y_ref = refs[:6]
        del refs[:6]
    else:
        o_ref, l_ref = refs[:2]
        del refs[:2]
    kr_ref, qs_ref, ks_ref = refs[:3]
    del refs[:3]
    if use_prev:
        kps_ref = refs.pop(0)
    if merge:
        ys_ref = refs.pop(0)

    n = pl.program_id(2)
    tile = q_ref.shape[0]
    scale = HEAD_DIM ** -0.5
    cos = cos_ref[...]
    sin = sin_ref[...]
    qs_ref[...] = _rope(q_ref[...], cos, sin)
    ks_ref[...] = _rope(kc_ref[...], cos, sin)
    kr_ref[...] = ks_ref[tile - keep:tile, :]
    if use_prev:
        kps_ref[...] = _rope(kp_ref[...], cosp_ref[...], sinp_ref[...])

    width = 2 * SPAN if use_prev else SPAN
    row = lax.broadcasted_iota(jnp.int32, (SPAN, width), 0)
    col = lax.broadcasted_iota(jnp.int32, (SPAN, width), 1)
    if use_prev:
        band = jnp.logical_or(jnp.logical_and(col < SPAN, col >= row),
                              jnp.logical_and(col >= SPAN, col - SPAN <= row))
        has_prev = jnp.zeros((SPAN, width), jnp.int32) + n > 0
        first = jnp.logical_and(band, jnp.logical_or(col >= SPAN, has_prev))
    else:
        band = col <= row
        first = band
    nt = (((1,), (1,)), ((), ()))

    def rows(j, r, count=SPAN):
        start = j * SPAN * dil + r
        return pl.ds(start, count) if dil == 1 else pl.ds(start, count, stride=dil)

    def load_kv(j, r):
        if not use_prev:
            return ks_ref[rows(j, r), :].astype(BF16), vc_ref[rows(j, r), :].astype(BF16)
        if j > 0:
            both = rows(j - 1, r, 2 * SPAN)
            return ks_ref[both, :].astype(BF16), vc_ref[both, :].astype(BF16)
        k2 = jnp.concatenate([kps_ref[rows(0, r), :], ks_ref[rows(0, r), :]], axis=0)
        v2 = jnp.concatenate([vp_ref[rows(0, r), :], vc_ref[rows(0, r), :]], axis=0)
        return k2.astype(BF16), v2.astype(BF16)

    def run(grp):
        qs = [qs_ref[rows(j, r), :].astype(BF16) for r, j in grp]
        kvs = [load_kv(j, r) for r, j in grp]
        ss = [lax.dot_general(q, k2, nt, preferred_element_type=F32) * scale
              for q, (k2, _) in zip(qs, kvs)]
        ss = [jnp.where(first if j == 0 else band, s, -jnp.inf) for s, (r, j) in zip(ss, grp)]
        mxs = [jnp.max(s, axis=1, keepdims=True) for s in ss]
        ps = [jnp.exp(s - mx) for s, mx in zip(ss, mxs)]
        dens = [jnp.sum(p, axis=1, keepdims=True) for p in ps]
        outs = [jnp.dot((p / den).astype(BF16), v2, preferred_element_type=F32)
                for p, den, (_, v2) in zip(ps, dens, kvs)]
        for (r, j), o, mx, den in zip(grp, outs, mxs, dens):
            idx = rows(j, r)
            lse = jnp.broadcast_to(mx + jnp.log(den), (SPAN, HEAD_DIM))
            if merge:
                om = _merge_groups([oa_ref[idx, :], ob_ref[idx, :], o],
                                   [la_ref[idx, :], lb_ref[idx, :], lse])
                ys_ref[idx, :] = om * _silu(cz_ref[idx, :])
            else:
                o_ref[idx, :] = o
                l_ref[idx, :] = lse

    order = [(r, j) for r in range(dil) for j in range(blocks)]
    for i in range(0, len(order), ATTN_ILP):
        run(order[i:i + ATTN_ILP])
    if merge:
        y_ref[...] = ys_ref[...].astype(BF16)


def _attn_group(h3, cos, sin, g, blocks, others=None):
    nb, s, _ = h3.shape
    dil = DILATIONS[g]
    unit = SPAN * dil
    tile = unit * blocks
    use_prev = s > unit
    per = tile // unit
    keep = min(WINDOWS[g], s)
    assert keep <= tile and s % tile == 0
    merge = others is not None
    head = lambda c0: (lambda hs: c0 // HEAD_DIM + g * HEADS_PER_GROUP + hs)
    cq, ck, cv = head(COL_Q), head(COL_K), head(COL_V)
    prev = lambda n: jnp.maximum(n * per - 1, 0)
    col_blk = lambda rows_, fn: pl.BlockSpec((None, rows_, HEAD_DIM), fn)
    tab_blk = lambda rows_, fn: pl.BlockSpec((rows_, HEAD_DIM), fn)
    own = lambda c: col_blk(tile, lambda b, hs, n: (b, n, c(hs)))
    in_specs = [own(cq), own(ck), own(cv),
                tab_blk(tile, lambda b, hs, n: (n, 0)), tab_blk(tile, lambda b, hs, n: (n, 0))]
    args = [h3, h3, h3, cos, sin]
    if use_prev:
        before = lambda c: col_blk(unit, lambda b, hs, n: (b, prev(n), c(hs)))
        in_specs += [before(ck), before(cv),
                     tab_blk(unit, lambda b, hs, n: (prev(n), 0)),
                     tab_blk(unit, lambda b, hs, n: (prev(n), 0))]
        args += [h3, h3, cos, sin]
    plain = col_blk(tile, lambda b, hs, n: (b, n, hs))
    if merge:
        in_specs += [plain] * 4 + [col_blk(tile, lambda b, hs, n: (b, n, COL_C_Z // HEAD_DIM + hs))]
        args += list(others) + [h3]
        out_specs = [plain]
        out_shape = [jax.ShapeDtypeStruct((nb, s, W_C), BF16)]
    else:
        out_specs = [plain, plain]
        out_shape = [jax.ShapeDtypeStruct((nb, s, W_C), F32)] * 2
    out_specs.append(col_blk(keep, lambda b, hs, n: (b, 0, hs)))
    out_shape.append(jax.ShapeDtypeStruct((nb, keep, W_C), F32))
    scratch = [pltpu.VMEM((tile, HEAD_DIM), F32), pltpu.VMEM((tile, HEAD_DIM), F32)]
    if use_prev:
        scratch.append(pltpu.VMEM((unit, HEAD_DIM), F32))
    if merge:
        scratch.append(pltpu.VMEM((tile, HEAD_DIM), F32))
    kern = functools.partial(_attn_kernel, dil=dil, blocks=blocks, use_prev=use_prev,
                             keep=keep, merge=merge)
    return pl.pallas_call(
        kern,
        grid=(nb, HEADS_PER_GROUP, s // tile),
        in_specs=in_specs,
        out_specs=out_specs,
        out_shape=out_shape,
        scratch_shapes=scratch,
        compiler_params=_params("parallel", "parallel", "arbitrary"),
        name=f"attn_g{g}",
    )(*args)


def _gate_out_kernel(x_ref, wg0, wg1, wg2, wg3, bg0, bg1, bg2, bg3,
                     ya, yb, yc, yd, woa, wob, woc, wod, o_ref):
    x = x_ref[...]
    acc = None
    for wg, bg, y, wo in ((wg0, bg0, ya, woa), (wg1, bg1, yb, wob),
                          (wg2, bg2, yc, woc), (wg3, bg3, yd, wod)):
        gate = jax.nn.sigmoid(jnp.dot(x, wg[...], preferred_element_type=F32) + bg[...])
        part = jnp.dot(y[...], wo[...], preferred_element_type=F32)
        acc = gate * part if acc is None else acc + gate * part
    o_ref[...] = acc.astype(BF16)


def _gate_out(xb, w_gate_b, b_gate, ys, wouts_b, l, tm, tn):
    m = xb.shape[0]
    nj = D_MODEL // tn
    wg_spec = lambda br: pl.BlockSpec((None, D_MODEL, tn), lambda i, j: (l, 0, br * nj + j))
    bg_spec = lambda br: pl.BlockSpec((None, 1, tn), lambda i, j: (l, 0, br * nj + j))
    y_spec = lambda y: pl.BlockSpec((tm, y.shape[1]), lambda i, j: (i, 0))
    wo_spec = lambda w: pl.BlockSpec((None, w.shape[1], tn), lambda i, j: (l, 0, j))
    bg3 = b_gate.reshape(DEPTH, 1, N_BRANCH * D_MODEL)
    return pl.pallas_call(
        _gate_out_kernel,
        grid=(m // tm, nj),
        in_specs=([pl.BlockSpec((tm, D_MODEL), lambda i, j: (i, 0))]
                  + [wg_spec(br) for br in range(N_BRANCH)]
                  + [bg_spec(br) for br in range(N_BRANCH)]
                  + [y_spec(y) for y in ys]
                  + [wo_spec(w) for w in wouts_b]),
        out_specs=pl.BlockSpec((tm, tn), lambda i, j: (i, j)),
        out_shape=jax.ShapeDtypeStruct((m, D_MODEL), BF16),
        compiler_params=_params("parallel", "parallel"),
        name="gate_out",
    )(xb, *([w_gate_b] * N_BRANCH), *([bg3] * N_BRANCH), *ys, *wouts_b)


def _proj_res_kernel(m_ref, wo_ref, x_ref, z_ref):
    z_ref[...] = ALPHA * x_ref[...] + jnp.dot(m_ref[...], wo_ref[...].astype(BF16),
                                              preferred_element_type=F32)


def _ln_rows_kernel(z_ref, g_ref, b_ref, y_ref, yb_ref):
    y = _layernorm(z_ref[...], g_ref[...], b_ref[...])
    y_ref[...] = y
    yb_ref[...] = y.astype(BF16)


def _out_ln(merged, w_o, x, ln_g, ln_b, l, tm, tn, tr):
    m = x.shape[0]
    z = pl.pallas_call(
        _proj_res_kernel,
        grid=(m // tm, D_MODEL // tn),
        in_specs=[pl.BlockSpec((tm, D_MODEL), lambda i, j: (i, 0)),
                  pl.BlockSpec((None, D_MODEL, tn), lambda i, j: (l, 0, j)),
                  pl.BlockSpec((tm, tn), lambda i, j: (i, j))],
        out_specs=pl.BlockSpec((tm, tn), lambda i, j: (i, j)),
        out_shape=jax.ShapeDtypeStruct((m, D_MODEL), F32),
        compiler_params=_params("parallel", "parallel"),
        name="proj_res",
    )(merged, w_o, x)
    return pl.pallas_call(
        _ln_rows_kernel,
        grid=(m // tr,),
        in_specs=[pl.BlockSpec((tr, D_MODEL), lambda i: (i, 0)),
                  _row_spec(l, D_MODEL, 1), _row_spec(l, D_MODEL, 1)],
        out_specs=[pl.BlockSpec((tr, D_MODEL), lambda i: (i, 0)),
                   pl.BlockSpec((tr, D_MODEL), lambda i: (i, 0))],
        out_shape=[jax.ShapeDtypeStruct((m, D_MODEL), F32),
                   jax.ShapeDtypeStruct((m, D_MODEL), BF16)],
        compiler_params=_params("parallel"),
        name="ln_rows",
    )(z, _row_param(ln_g), _row_param(ln_b))


def _sample_mix_kernel(h_ref, ha_ref, hb_ref, wa_ref, wb_ref, bias_ref, lbg_ref, lbb_ref,
                       ldg_ref, ldb_ref, ws0_ref, bs0_ref, cos_ref, sin_ref,
                       ya_ref, yb_ref, yd_ref, na_ref, nb_ref, vn_ref, qk_ref):
    col = lambda c, w: h_ref[:, c:c + w]
    u_a = col(COL_A_C, W_A) * col(COL_A_H, W_A)
    conv_a = wa_ref[0:1, :] * ha_ref[0] + wa_ref[1:2, :] * ha_ref[1] + wa_ref[2:3, :] * u_a
    ya_ref[...] = col(COL_A_B, W_A) * conv_a * _silu(col(COL_A_Z, W_A))
    na_ref[0] = ha_ref[1]
    na_ref[1] = u_a
    u_b = col(COL_B_A, W_B) * jax.nn.sigmoid(col(COL_B_G, W_B))
    cb = wb_ref[CONV_B - 1:CONV_B, :] * u_b + bias_ref[...]
    for k in range(CONV_B - 1):
        cb = cb + wb_ref[k:k + 1, :] * hb_ref[k]
    yb_ref[...] = _silu(_layernorm(cb, lbg_ref[...], lbb_ref[...])) * _silu(col(COL_B_Z, W_B))
    for k in range(CONV_B - 2):
        nb_ref[k] = hb_ref[k + 1]
    nb_ref[CONV_B - 2] = u_b
    vn = _layernorm(col(COL_D_V, W_D), ldg_ref[...], ldb_ref[...])
    vn_ref[...] = vn
    mix = ws0_ref[...] * vn + bs0_ref[...]
    yd_ref[...] = col(COL_D_U, W_D) * mix * _silu(col(COL_D_Z, W_D))
    cos = cos_ref[...]
    sin = sin_ref[...]
    for hh in range(2 * N_HEADS):
        cs = slice(hh * HEAD_DIM, (hh + 1) * HEAD_DIM)
        qk_ref[:, cs] = _rope(h_ref[:, COL_Q + hh * HEAD_DIM:COL_Q + (hh + 1) * HEAD_DIM], cos, sin)


def _sample_mix(hs, hist_a, hist_b, conv_a_w, conv_b_w, conv_b_bias, ln_b_g, ln_b_b,
                ln_d_g, ln_d_b, w_s, b_s, cos, sin, l):
    r = SAMPLE_ROWS
    ws0 = jnp.repeat(w_s[l, :, 0, 0], LANES)[None, :]
    bs0 = jnp.repeat(b_s[l, :, 0], LANES)[None, :]
    row = lambda p: p[l][None, :]
    f = lambda shape: jax.ShapeDtypeStruct(shape, F32)
    return pl.pallas_call(
        _sample_mix_kernel,
        out_shape=[f((r, W_A)), f((r, W_B)), f((r, W_D)), f((CONV_A - 1, r, W_A)),
                   f((CONV_B - 1, r, W_B)), f((r, W_D)), f((r, 2 * W_QKV))],
        compiler_params=pltpu.CompilerParams(vmem_limit_bytes=VMEM_LIMIT_BYTES),
        name="sample_mix",
    )(hs, hist_a, hist_b, conv_a_w[l], conv_b_w[l], row(conv_b_bias), row(ln_b_g), row(ln_b_b),
      row(ln_d_g), row(ln_d_b), ws0, bs0, cos, sin)


def _sample_attn_kernel(qk_ref, h_ref, *refs):
    y_ref = refs[-1]
    per_group = 2 * HEADS_PER_GROUP
    b = pl.program_id(0)
    scale = HEAD_DIM ** -0.5

    @pl.when(b == 0)
    def _():
        y_ref[...] = jnp.zeros(y_ref.shape, F32)

    is_b = lax.broadcasted_iota(jnp.int32, (y_ref.shape[0], HEAD_DIM), 0) == b

    def pick(ref, c0):
        return jnp.sum(jnp.where(is_b, ref[:, c0:c0 + HEAD_DIM], 0.0), axis=0, keepdims=True)

    for hs in range(HEADS_PER_GROUP):
        os_, ls_ = [], []
        for g in range(N_GROUPS):
            dil = DILATIONS[g]
            hcol = (g * HEADS_PER_GROUP + hs) * HEAD_DIM
            q = pick(qk_ref, hcol)
            k_new = pick(qk_ref, W_QKV + hcol)
            v_new = pick(h_ref, COL_V + hcol)
            kc_ref = refs[g * per_group + hs]
            vc_ref = refs[g * per_group + HEADS_PER_GROUP + hs]
            start = kc_ref.shape[0] - SPAN * dil
            ridx = pl.ds(start, SPAN) if dil == 1 else pl.ds(start, SPAN, stride=dil)
            kc = kc_ref[ridx, :]
            vc = vc_ref[ridx, :]
            s_c = jnp.sum(kc * q, axis=1, keepdims=True) * scale
            s_n = jnp.sum(k_new * q, axis=1, keepdims=True) * scale
            mx = jnp.maximum(jnp.max(s_c, axis=0, keepdims=True), s_n)
            p_c = jnp.exp(s_c - mx)
            p_n = jnp.exp(s_n - mx)
            den = jnp.sum(p_c, axis=0, keepdims=True) + p_n
            o = jnp.sum((p_c / den) * vc, axis=0, keepdims=True) + (p_n / den) * v_new
            os_.append(o)
            ls_.append(mx + jnp.log(den))
        o = _merge_groups(os_, ls_)
        y = o * _silu(pick(h_ref, COL_C_Z + hs * HEAD_DIM))
        cs = slice(hs * HEAD_DIM, (hs + 1) * HEAD_DIM)
        y_ref[:, cs] = jnp.where(is_b, jnp.broadcast_to(y, is_b.shape), y_ref[:, cs])


def _sample_attn(qk, hs, caches, l, nb):
    r = SAMPLE_ROWS
    cspec = lambda c, j: pl.BlockSpec((None, None, c.shape[2], HEAD_DIM), lambda b: (l, b, 0, j))
    cache_specs, cache_args = [], []
    for c in caches:
        for j in range(2 * HEADS_PER_GROUP):
            cache_specs.append(cspec(c, j))
            cache_args.append(c)
    return pl.pallas_call(
        _sample_attn_kernel,
        grid=(nb,),
        in_specs=[pl.BlockSpec((r, 2 * W_QKV), lambda b: (0, 0)),
                  pl.BlockSpec((r, N_BRANCH_COLS), lambda b: (0, 0))] + cache_specs,
        out_specs=pl.BlockSpec((r, W_C), lambda b: (0, 0)),
        out_shape=jax.ShapeDtypeStruct((r, W_C), F32),
        compiler_params=_params("arbitrary"),
        name="sample_attn",
    )(qk, hs, *cache_args)


def _kv_rows(k_rot, h3, g):
    nb, keep, _ = k_rot.shape
    c0 = COL_V + g * W_C
    v = h3[:, -keep:, c0:c0 + W_C]
    return jnp.stack([k_rot, v], axis=2).reshape(nb, keep, 2, HEADS_PER_GROUP, HEAD_DIM)


def kernel(x_prompt, x_sample, state_conv_a, state_conv_b, cache_kv_w128, cache_kv_w512,
           cache_kv_w2048, w_in, conv_a_w, conv_b_w, conv_b_bias, ln_b_g, ln_b_b, ln_d_g,
           ln_d_b, w_s, b_s, w_out_a, w_out_b, w_out_c, w_out_d, b_gate, w_o, ln_g, ln_b):
    bp, tp, _ = x_prompt.shape
    bs, ts, _ = x_sample.shape
    assert ts == 1 and bs <= SAMPLE_ROWS
    mp = bp * tp
    pad = SAMPLE_ROWS - bs

    w_gate_b = w_in[:, :, COL_GATE:].astype(BF16)
    wouts_b = [w.astype(BF16) for w in (w_out_a, w_out_b, w_out_c, w_out_d)]
    caches = [c.reshape(c.shape[0], c.shape[1], c.shape[2], 2 * W_C)
              for c in (cache_kv_w128, cache_kv_w512, cache_kv_w2048)]

    cos_p, sin_p = _rope_tables(jnp.arange(tp, dtype=jnp.int32))
    cos_s, sin_s = _rope_tables(PAST_LEN + jnp.arange(ts, dtype=jnp.int32))

    xp = x_prompt.reshape(mp, D_MODEL)
    xpb = xp.astype(BF16)
    xs = jnp.pad(x_sample.reshape(bs, D_MODEL), ((0, pad), (0, 0)))
    xsb = xs.astype(BF16)

    ca_p, ca_s, cb_p, cb_s, dv_s = [], [], [], [], []
    kv_p = [[] for _ in range(N_GROUPS)]
    kv_s = [[] for _ in range(N_GROUPS)]
    for l in range(DEPTH):
        h3 = _in_proj(xpb, w_in, l, 2048, 512).reshape(bp, tp, N_BRANCH_COLS)
        y_a, st_a = _branch_a(h3, conv_a_w, l, 512)
        y_b, st_b = _branch_b(h3, conv_b_w, conv_b_bias, ln_b_g, ln_b_b, l, 256)
        y_d = _branch_d(h3, ln_d_g, ln_d_b, w_s, b_s, l, 512)
        o0, l0, kr0 = _attn_group(h3, cos_p, sin_p, 0, 4)
        o1, l1, kr1 = _attn_group(h3, cos_p, sin_p, 1, 1)
        y_c, kr2 = _attn_group(h3, cos_p, sin_p, 2, 1, others=(o0, l0, o1, l1))
        ys = [y.reshape(mp, y.shape[-1]) for y in (y_a, y_b, y_c, y_d)]
        merged = _gate_out(xpb, w_gate_b, b_gate, ys, wouts_b, l, 512, 256)
        xp, xpb = _out_ln(merged, w_o, xp, ln_g, ln_b, l, 1024, 512, 256)
        ca_p.append(st_a)
        cb_p.append(st_b)
        for g, kr in enumerate((kr0, kr1, kr2)):
            kv_p[g].append(_kv_rows(kr, h3, g))

        hs = _in_proj(xsb, w_in, l, SAMPLE_ROWS, 512)
        hist_a = jnp.pad(state_conv_a[l].transpose(1, 0, 2), ((0, 0), (0, pad), (0, 0)))
        hist_b = jnp.pad(state_conv_b[l].transpose(1, 0, 2), ((0, 0), (0, pad), (0, 0)))
        ya_s, yb_s, yd_s, na, nb_, vn, qk_s = _sample_mix(
            hs, hist_a, hist_b, conv_a_w, conv_b_w, conv_b_bias, ln_b_g, ln_b_b,
            ln_d_g, ln_d_b, w_s, b_s, cos_s, sin_s, l)
        yc_s = _sample_attn(qk_s, hs, caches, l, bs)
        ys_s = [y.astype(BF16) for y in (ya_s, yb_s, yc_s, yd_s)]
        merged_s = _gate_out(xsb, w_gate_b, b_gate, ys_s, wouts_b, l, SAMPLE_ROWS, 256)
        xs, xsb = _out_ln(merged_s, w_o, xs, ln_g, ln_b, l, SAMPLE_ROWS, 512, SAMPLE_ROWS)
        ca_s.append(na[:, :bs].transpose(1, 0, 2))
        cb_s.append(nb_[:, :bs].transpose(1, 0, 2))
        dv_s.append(vn[:bs].reshape(bs, ts, W_D))
        for g in range(N_GROUPS):
            c0 = g * W_C
            k = qk_s[:bs, W_QKV + c0:W_QKV + c0 + W_C]
            v = hs[:bs, COL_V + c0:COL_V + c0 + W_C]
            kv_s[g].append(jnp.stack([k, v], axis=1).reshape(bs, ts, 2, HEADS_PER_GROUP, HEAD_DIM))

    return (xp.reshape(bp, tp, D_MODEL), xs[:bs].reshape(bs, ts, D_MODEL),
            jnp.stack(ca_p), jnp.stack(ca_s), jnp.stack(cb_p), jnp.stack(cb_s),
            jnp.stack(kv_p[0]), jnp.stack(kv_s[0]), jnp.stack(kv_p[1]), jnp.stack(kv_s[1]),
            jnp.stack(kv_p[2]), jnp.stack(kv_s[2]), jnp.stack(dv_s))
```

```python
import functools

import jax
import jax.numpy as jnp
from jax import lax
from jax.experimental import pallas as pl
from jax.experimental.pallas import tpu as pltpu

D_MODEL = 4096
DEPTH = 2
PAST_LEN = 16384
W_A = 1024
W_B = 1024
HEAD_DIM = 128
N_GROUPS = 3
HEADS_PER_GROUP = 4
N_HEADS = N_GROUPS * HEADS_PER_GROUP
W_QKV = N_HEADS * HEAD_DIM
W_C = HEADS_PER_GROUP * HEAD_DIM
W_D = 1024
CHUNK = 128
D_GROUPS = 8
CONV_A = 3
CONV_B = 31
WINDOWS = (128, 512, 2048)
DILATIONS = (1, 4, 16)
SPAN = 128
ROPE_THETA = 10000.0
N_BRANCH = 4
LN_EPS = 1e-5
ALPHA = (2 * DEPTH) ** 0.25

COL_A_B, COL_A_C, COL_A_H, COL_A_Z = 0, 1024, 2048, 3072
COL_B_A, COL_B_G, COL_B_Z = 4096, 5120, 6144
COL_Q, COL_K, COL_V, COL_C_Z = 7168, 8704, 10240, 11776
COL_D_U, COL_D_V, COL_D_Z = 12288, 13312, 14336
N_BRANCH_COLS = 15360
COL_GATE = 15360

LANES = 128
SUBLANES = 8
SAMPLE_BLOCK = 64
VMEM_LIMIT_BYTES = 56 * 1024 * 1024

F32 = jnp.float32
BF16 = jnp.bfloat16


def _params(*sem):
    return pltpu.CompilerParams(dimension_semantics=sem, vmem_limit_bytes=VMEM_LIMIT_BYTES)


def _silu(v):
    return v * jax.nn.sigmoid(v)


def _layernorm(v, g, b):
    mu = jnp.mean(v, axis=-1, keepdims=True)
    c = v - mu
    var = jnp.mean(c * c, axis=-1, keepdims=True)
    return c * lax.rsqrt(var + LN_EPS) * g + b


def _row_param(p):
    return p.reshape(DEPTH, 1, p.shape[-1])


def _row_spec(l, c, nargs):
    if nargs == 1:
        return pl.BlockSpec((None, 1, c), lambda b: (l, 0, 0))
    return pl.BlockSpec((None, 1, c), lambda b, t: (l, 0, 0))


def _tiles(m_all):
    def largest(limit, mult):
        return max(t for t in range(mult, limit + 1, mult) if m_all % t == 0)
    return dict(in_proj=largest(2176, 16), gate_out=largest(768, 16),
                proj_res=largest(1536, 16), ln_rows=SAMPLE_BLOCK)


def _mm_kernel(x_ref, w_ref, o_ref):
    o_ref[...] = jnp.dot(x_ref[...], w_ref[...].astype(BF16), preferred_element_type=F32)


def _in_proj(xb, w_in, l, tm, tn):
    m = xb.shape[0]
    return pl.pallas_call(
        _mm_kernel,
        grid=(m // tm, N_BRANCH_COLS // tn),
        in_specs=[pl.BlockSpec((tm, D_MODEL), lambda i, j: (i, 0), pipeline_mode=pl.Buffered(1)),
                  pl.BlockSpec((None, D_MODEL, tn), lambda i, j: (l, 0, j))],
        out_specs=pl.BlockSpec((tm, tn), lambda i, j: (i, j)),
        out_shape=jax.ShapeDtypeStruct((m, N_BRANCH_COLS), F32),
        compiler_params=_params("parallel", "parallel"),
        name="in_proj",
    )(xb, w_in)


A_PAD = 8


def _branch_a_kernel(ab_ref, ac_ref, ah_ref, az_ref, w_ref, y_ref, st_ref, ext_ref):
    t = pl.program_id(1)
    tt = ab_ref.shape[0]

    @pl.when(t == 0)
    def _():
        ext_ref[0:A_PAD, :] = jnp.zeros((A_PAD, W_A), F32)

    @pl.when(t > 0)
    def _():
        ext_ref[0:A_PAD, :] = ext_ref[tt:tt + A_PAD, :]

    u = ac_ref[...] * ah_ref[...]
    ext_ref[A_PAD:A_PAD + tt, :] = u
    w = w_ref[...]
    conv = (w[0:1, :] * ext_ref[A_PAD - 2:A_PAD - 2 + tt, :]
            + w[1:2, :] * ext_ref[A_PAD - 1:A_PAD - 1 + tt, :]
            + w[2:3, :] * u)
    y_ref[...] = (ab_ref[...] * conv * _silu(az_ref[...])).astype(BF16)
    st_ref[...] = ext_ref[A_PAD + tt - (CONV_A - 1):A_PAD + tt, :]


def _branch_a(h, nb, s, conv_a_w, l, tt):
    nt = s // tt
    col = lambda c: pl.BlockSpec((tt, W_A), lambda b, t: (b * nt + t, c // W_A))
    return pl.pallas_call(
        _branch_a_kernel,
        grid=(nb, nt),
        in_specs=[col(COL_A_B), col(COL_A_C), col(COL_A_H), col(COL_A_Z),
                  pl.BlockSpec((None, CONV_A, W_A), lambda b, t: (l, 0, 0))],
        out_specs=[pl.BlockSpec((tt, W_A), lambda b, t: (b * nt + t, 0)),
                   pl.BlockSpec((None, CONV_A - 1, W_A), lambda b, t: (b, 0, 0))],
        out_shape=[jax.ShapeDtypeStruct((h.shape[0], W_A), BF16),
                   jax.ShapeDtypeStruct((nb, CONV_A - 1, W_A), F32)],
        scratch_shapes=[pltpu.VMEM((A_PAD + tt, W_A), F32)],
        compiler_params=_params("parallel", "arbitrary"),
        name="branch_a",
    )(h, h, h, h, conv_a_w)


B_PAD = 32
B_ROWS = 64


def _branch_b_kernel(ba_ref, bg_ref, bz_ref, w_ref, bias_ref, g_ref, b_ref,
                     y_ref, st_ref, ext_ref, sh_ref, cb_ref):
    t = pl.program_id(1)
    tt = ba_ref.shape[0]

    @pl.when(t == 0)
    def _():
        ext_ref[0:B_PAD, :] = jnp.zeros((B_PAD, W_B), F32)

    @pl.when(t > 0)
    def _():
        ext_ref[0:B_PAD, :] = ext_ref[tt:tt + B_PAD, :]

    ext_ref[B_PAD:B_PAD + tt, :] = ba_ref[...] * jax.nn.sigmoid(bg_ref[...])
    sh_rows = sh_ref.shape[1]
    for s in range(1, SUBLANES):
        sh_ref[s - 1] = ext_ref[s:s + sh_rows, :]
    first = B_PAD - (CONV_B - 1)
    for c in range(W_B // LANES):
        cs = slice(c * LANES, (c + 1) * LANES)
        wc = w_ref[:, cs]
        for rc in range(tt // B_ROWS):
            acc = jnp.zeros((B_ROWS, LANES), F32)
            for k in range(CONV_B):
                q, s = divmod(first + k, SUBLANES)
                r0 = SUBLANES * q + rc * B_ROWS
                if s == 0:
                    tap = ext_ref[r0:r0 + B_ROWS, cs]
                else:
                    tap = sh_ref[s - 1, r0:r0 + B_ROWS, cs]
                acc = acc + wc[k:k + 1, :] * tap
            cb_ref[rc * B_ROWS:(rc + 1) * B_ROWS, cs] = acc
    cb = cb_ref[...] + bias_ref[...]
    ln = _layernorm(cb, g_ref[...], b_ref[...])
    y_ref[...] = (_silu(ln) * _silu(bz_ref[...])).astype(BF16)
    st_ref[...] = ext_ref[B_PAD + tt - (CONV_B - 1):B_PAD + tt, :]


def _branch_b(h, nb, s, conv_b_w, conv_b_bias, ln_b_g, ln_b_b, l, tt):
    nt = s // tt
    col = lambda c: pl.BlockSpec((tt, W_B), lambda b, t: (b * nt + t, c // W_B))
    return pl.pallas_call(
        _branch_b_kernel,
        grid=(nb, nt),
        in_specs=[col(COL_B_A), col(COL_B_G), col(COL_B_Z),
                  pl.BlockSpec((None, CONV_B, W_B), lambda b, t: (l, 0, 0)),
                  _row_spec(l, W_B, 2), _row_spec(l, W_B, 2), _row_spec(l, W_B, 2)],
        out_specs=[pl.BlockSpec((tt, W_B), lambda b, t: (b * nt + t, 0)),
                   pl.BlockSpec((None, CONV_B - 1, W_B), lambda b, t: (b, 0, 0))],
        out_shape=[jax.ShapeDtypeStruct((h.shape[0], W_B), BF16),
                   jax.ShapeDtypeStruct((nb, CONV_B - 1, W_B), F32)],
        scratch_shapes=[pltpu.VMEM((B_PAD + tt, W_B), F32),
                        pltpu.VMEM((SUBLANES - 1, B_PAD + tt - SUBLANES, W_B), F32),
                        pltpu.VMEM((tt, W_B), F32)],
        compiler_params=_params("parallel", "arbitrary"),
        name="branch_b",
    )(h, h, h, conv_b_w, _row_param(conv_b_bias), _row_param(ln_b_g), _row_param(ln_b_b))


def _branch_d_kernel(du_ref, dv_ref, dz_ref, g_ref, b_ref, ws_ref, bs_ref, y_ref):
    tt = du_ref.shape[0]
    vb = _layernorm(dv_ref[...], g_ref[...], b_ref[...]).astype(BF16)
    row = lax.broadcasted_iota(jnp.int32, (CHUNK, CHUNK), 0)
    col = lax.broadcasted_iota(jnp.int32, (CHUNK, CHUNK), 1)
    causal = col <= row
    for g in range(D_GROUPS):
        cs = slice(g * LANES, (g + 1) * LANES)
        wm = jnp.where(causal, ws_ref[g], 0.0).astype(BF16)
        for c in range(tt // CHUNK):
            rs = slice(c * CHUNK, (c + 1) * CHUNK)
            mix = jnp.dot(wm, vb[rs, cs], preferred_element_type=F32) + bs_ref[:, cs]
            y_ref[rs, cs] = (du_ref[rs, cs] * mix * _silu(dz_ref[rs, cs])).astype(BF16)


def _bias_slab(b_s_l):
    return jnp.repeat(b_s_l.T, LANES, axis=1)


def _branch_d(h, nb, s, ln_d_g, ln_d_b, w_s, b_s, l, tt):
    nt = s // tt
    col = lambda c: pl.BlockSpec((tt, W_D), lambda b, t: (b * nt + t, c // W_D))
    return pl.pallas_call(
        _branch_d_kernel,
        grid=(nb, nt),
        in_specs=[col(COL_D_U), col(COL_D_V), col(COL_D_Z),
                  _row_spec(l, W_D, 2), _row_spec(l, W_D, 2),
                  pl.BlockSpec((None, D_GROUPS, CHUNK, CHUNK), lambda b, t: (l, 0, 0, 0)),
                  pl.BlockSpec((CHUNK, W_D), lambda b, t: (0, 0))],
        out_specs=pl.BlockSpec((tt, W_D), lambda b, t: (b * nt + t, 0)),
        out_shape=jax.ShapeDtypeStruct((h.shape[0], W_D), BF16),
        compiler_params=_params("parallel", "parallel"),
        name="branch_d",
    )(h, h, h, _row_param(ln_d_g), _row_param(ln_d_b), w_s, _bias_slab(b_s[l]))


def _rope_tables(pos):
    half = HEAD_DIM // 2
    inv = ROPE_THETA ** (-jnp.arange(half, dtype=F32) / half)
    ang = pos.astype(F32)[:, None] * inv[None, :]
    cos = jnp.cos(ang)
    sin = jnp.sin(ang)
    return jnp.concatenate([cos, cos], axis=-1), jnp.concatenate([-sin, sin], axis=-1)


def _rope(v, cos, sin):
    return v * cos + pltpu.roll(v, HEAD_DIM // 2, v.ndim - 1) * sin


def _merge_groups(os_, ls_):
    mx = jnp.maximum(jnp.maximum(ls_[0], ls_[1]), ls_[2])
    es = [jnp.exp(l_ - mx) for l_ in ls_]
    den = es[0] + es[1] + es[2]
    return (es[0] / den) * os_[0] + (es[1] / den) * os_[1] + (es[2] / den) * os_[2]


ATTN_ILP = 2


def _attn_kernel(*refs, dil, blocks, use_prev, keep, merge):
    refs = list(refs)
    q_ref, kc_ref, vc_ref, cos_ref, sin_ref = refs[:5]
    del refs[:5]
    if use_prev:
        kp_ref, vp_ref, cosp_ref, sinp_ref = refs[:4]
        del refs[:4]
    if merge:
        oa_ref, la_ref, ob_ref, lb_ref, cz_ref, y_ref = refs[:6]
        del refs[:6]
    else:
        o_ref, l_ref = refs[:2]
        del refs[:2]
    kr_ref, qs_ref, ks_ref = refs[:3]
    del refs[:3]
    if use_prev:
        kps_ref = refs.pop(0)
    if merge:
        ys_ref = refs.pop(0)

    n = pl.program_id(2)
    tile = q_ref.shape[0]
    scale = HEAD_DIM ** -0.5
    cos = cos_ref[...]
    sin = sin_ref[...]
    qs_ref[...] = _rope(q_ref[...], cos, sin)
    ks_ref[...] = _rope(kc_ref[...], cos, sin)
    kr_ref[...] = ks_ref[tile - keep:tile, :]
    if use_prev:
        kps_ref[...] = _rope(kp_ref[...], cosp_ref[...], sinp_ref[...])

    width = 2 * SPAN if use_prev else SPAN
    row = lax.broadcasted_iota(jnp.int32, (SPAN, width), 0)
    col = lax.broadcasted_iota(jnp.int32, (SPAN, width), 1)
    if use_prev:
        band = jnp.logical_or(jnp.logical_and(col < SPAN, col >= row),
                              jnp.logical_and(col >= SPAN, col - SPAN <= row))
        has_prev = jnp.zeros((SPAN, width), jnp.int32) + n > 0
        first = jnp.logical_and(band, jnp.logical_or(col >= SPAN, has_prev))
    else:
        band = col <= row
        first = band
    nt = (((1,), (1,)), ((), ()))

    def rows(j, r, count=SPAN):
        start = j * SPAN * dil + r
        return pl.ds(start, count) if dil == 1 else pl.ds(start, count, stride=dil)

    def load_kv(j, r):
        if not use_prev:
            return ks_ref[rows(j, r), :].astype(BF16), vc_ref[rows(j, r), :].astype(BF16)
        if j > 0:
            both = rows(j - 1, r, 2 * SPAN)
            return ks_ref[both, :].astype(BF16), vc_ref[both, :].astype(BF16)
        k2 = jnp.concatenate([kps_ref[rows(0, r), :], ks_ref[rows(0, r), :]], axis=0)
        v2 = jnp.concatenate([vp_ref[rows(0, r), :], vc_ref[rows(0, r), :]], axis=0)
        return k2.astype(BF16), v2.astype(BF16)

    def run(grp):
        qs = [qs_ref[rows(j, r), :].astype(BF16) for r, j in grp]
        kvs = [load_kv(j, r) for r, j in grp]
        ss = [lax.dot_general(q, k2, nt, preferred_element_type=F32) * scale
              for q, (k2, _) in zip(qs, kvs)]
        ss = [jnp.where(first if j == 0 else band, s, -jnp.inf) for s, (r, j) in zip(ss, grp)]
        mxs = [jnp.max(s, axis=1, keepdims=True) for s in ss]
        ps = [jnp.exp(s - mx) for s, mx in zip(ss, mxs)]
        dens = [jnp.sum(p, axis=1, keepdims=True) for p in ps]
        outs = [jnp.dot((p / den).astype(BF16), v2, preferred_element_type=F32)
                for p, den, (_, v2) in zip(ps, dens, kvs)]
        for (r, j), o, mx, den in zip(grp, outs, mxs, dens):
            idx = rows(j, r)
            lse = jnp.broadcast_to(mx + jnp.log(den), (SPAN, HEAD_DIM))
            if merge:
                om = _merge_groups([oa_ref[idx, :], ob_ref[idx, :], o],
                                   [la_ref[idx, :], lb_ref[idx, :], lse])
                ys_ref[idx, :] = om * _silu(cz_ref[idx, :])
            else:
                o_ref[idx, :] = o
                l_ref[idx, :] = lse

    order = [(r, j) for r in range(dil) for j in range(blocks)]
    for i in range(0, len(order), ATTN_ILP):
        run(order[i:i + ATTN_ILP])
    if merge:
        y_ref[...] = ys_ref[...].astype(BF16)


def _attn_group(h, nb, s, cos, sin, g, blocks, others=None):
    dil = DILATIONS[g]
    unit = SPAN * dil
    tile = unit * blocks
    use_prev = s > unit
    per = tile // unit
    nt, nu = s // tile, s // unit
    keep = min(WINDOWS[g], s)
    assert keep <= tile and s % tile == 0
    merge = others is not None
    head = lambda c0: (lambda hs: c0 // HEAD_DIM + g * HEADS_PER_GROUP + hs)
    cq, ck, cv = head(COL_Q), head(COL_K), head(COL_V)
    prev = lambda n: jnp.maximum(n * per - 1, 0)
    col_blk = lambda rows_, fn: pl.BlockSpec((rows_, HEAD_DIM), fn)
    own = lambda c: col_blk(tile, lambda b, hs, n: (b * nt + n, c(hs)))
    in_specs = [own(cq), own(ck), own(cv),
                col_blk(tile, lambda b, hs, n: (n, 0)), col_blk(tile, lambda b, hs, n: (n, 0))]
    args = [h, h, h, cos, sin]
    if use_prev:
        before = lambda c: col_blk(unit, lambda b, hs, n: (b * nu + prev(n), c(hs)))
        in_specs += [before(ck), before(cv),
                     col_blk(unit, lambda b, hs, n: (prev(n), 0)),
                     col_blk(unit, lambda b, hs, n: (prev(n), 0))]
        args += [h, h, cos, sin]
    plain = col_blk(tile, lambda b, hs, n: (b * nt + n, hs))
    if merge:
        in_specs += [plain] * 4 + [own(lambda hs: COL_C_Z // HEAD_DIM + hs)]
        args += list(others) + [h]
        out_specs = [plain]
        out_shape = [jax.ShapeDtypeStruct((h.shape[0], W_C), BF16)]
    else:
        out_specs = [plain, plain]
        out_shape = [jax.ShapeDtypeStruct((nb * s, W_C), F32)] * 2
    out_specs.append(pl.BlockSpec((None, keep, HEAD_DIM), lambda b, hs, n: (b, 0, hs)))
    out_shape.append(jax.ShapeDtypeStruct((nb, keep, W_C), F32))
    scratch = [pltpu.VMEM((tile, HEAD_DIM), F32), pltpu.VMEM((tile, HEAD_DIM), F32)]
    if use_prev:
        scratch.append(pltpu.VMEM((unit, HEAD_DIM), F32))
    if merge:
        scratch.append(pltpu.VMEM((tile, HEAD_DIM), F32))
    kern = functools.partial(_attn_kernel, dil=dil, blocks=blocks, use_prev=use_prev,
                             keep=keep, merge=merge)
    return pl.pallas_call(
        kern,
        grid=(nb, HEADS_PER_GROUP, nt),
        in_specs=in_specs,
        out_specs=out_specs,
        out_shape=out_shape,
        scratch_shapes=scratch,
        compiler_params=_params("parallel", "parallel", "arbitrary"),
        name=f"attn_g{g}",
    )(*args)


def _gate_out_kernel(x_ref, wg0, wg1, wg2, wg3, bg0, bg1, bg2, bg3,
                     ya, yb, yc, yd, woa, wob, woc, wod, o_ref):
    x = x_ref[...]
    acc = None
    for wg, bg, y, wo in ((wg0, bg0, ya, woa), (wg1, bg1, yb, wob),
                          (wg2, bg2, yc, woc), (wg3, bg3, yd, wod)):
        gate = jax.nn.sigmoid(jnp.dot(x, wg[...], preferred_element_type=F32) + bg[...])
        part = jnp.dot(y[...], wo[...], preferred_element_type=F32)
        acc = gate * part if acc is None else acc + gate * part
    o_ref[...] = acc.astype(BF16)


def _gate_out(xb, w_gate_b, b_gate, ys, wouts_b, l, tm, tn):
    m = xb.shape[0]
    nj = D_MODEL // tn
    wg_spec = lambda br: pl.BlockSpec((None, D_MODEL, tn), lambda i, j: (l, 0, br * nj + j))
    bg_spec = lambda br: pl.BlockSpec((None, 1, tn), lambda i, j: (l, 0, br * nj + j))
    y_spec = lambda y: pl.BlockSpec((tm, y.shape[1]), lambda i, j: (i, 0))
    wo_spec = lambda w: pl.BlockSpec((None, w.shape[1], tn), lambda i, j: (l, 0, j))
    bg3 = b_gate.reshape(DEPTH, 1, N_BRANCH * D_MODEL)
    return pl.pallas_call(
        _gate_out_kernel,
        grid=(m // tm, nj),
        in_specs=([pl.BlockSpec((tm, D_MODEL), lambda i, j: (i, 0))]
                  + [wg_spec(br) for br in range(N_BRANCH)]
                  + [bg_spec(br) for br in range(N_BRANCH)]
                  + [y_spec(y) for y in ys]
                  + [wo_spec(w) for w in wouts_b]),
        out_specs=pl.BlockSpec((tm, tn), lambda i, j: (i, j)),
        out_shape=jax.ShapeDtypeStruct((m, D_MODEL), BF16),
        compiler_params=_params("parallel", "parallel"),
        name="gate_out",
    )(xb, *([w_gate_b] * N_BRANCH), *([bg3] * N_BRANCH), *ys, *wouts_b)


def _proj_res_kernel(m_ref, wo_ref, x_ref, z_ref):
    z_ref[...] = ALPHA * x_ref[...] + jnp.dot(m_ref[...], wo_ref[...], preferred_element_type=F32)


def _proj_res(merged, w_o_b, x, l, tm, tn):
    m = x.shape[0]
    return pl.pallas_call(
        _proj_res_kernel,
        grid=(m // tm, D_MODEL // tn),
        in_specs=[pl.BlockSpec((tm, D_MODEL), lambda i, j: (i, 0)),
                  pl.BlockSpec((None, D_MODEL, tn), lambda i, j: (l, 0, j)),
                  pl.BlockSpec((tm, tn), lambda i, j: (i, j))],
        out_specs=pl.BlockSpec((tm, tn), lambda i, j: (i, j)),
        out_shape=jax.ShapeDtypeStruct((m, D_MODEL), F32),
        compiler_params=_params("parallel", "parallel"),
        name="proj_res",
    )(merged, w_o_b, x)


def _ln_rows_kernel(z_ref, g_ref, b_ref, y_ref, yb_ref):
    y = _layernorm(z_ref[...], g_ref[...], b_ref[...])
    y_ref[...] = y
    yb_ref[...] = y.astype(BF16)


def _ln_rows(z, ln_g, ln_b, l, tr):
    m = z.shape[0]
    blk = pl.BlockSpec((tr, D_MODEL), lambda i: (i, 0))
    return pl.pallas_call(
        _ln_rows_kernel,
        grid=(m // tr,),
        in_specs=[blk, _row_spec(l, D_MODEL, 1), _row_spec(l, D_MODEL, 1)],
        out_specs=[blk, blk],
        out_shape=[jax.ShapeDtypeStruct((m, D_MODEL), F32),
                   jax.ShapeDtypeStruct((m, D_MODEL), BF16)],
        compiler_params=_params("parallel"),
        name="ln_rows",
    )(z, _row_param(ln_g), _row_param(ln_b))


def _ln_split_kernel(z_ref, g_ref, b_ref, yp_ref, ys_ref, *, n_prompt):
    i = pl.program_id(0)
    y = _layernorm(z_ref[...], g_ref[...], b_ref[...])

    @pl.when(i < n_prompt)
    def _():
        yp_ref[...] = y

    @pl.when(i >= n_prompt)
    def _():
        ys_ref[...] = y


def _ln_rows_split(z, ln_g, ln_b, l, m_prompt):
    tr = SAMPLE_BLOCK
    assert z.shape[0] == m_prompt + tr and m_prompt % tr == 0
    n_prompt = m_prompt // tr
    return pl.pallas_call(
        functools.partial(_ln_split_kernel, n_prompt=n_prompt),
        grid=(n_prompt + 1,),
        in_specs=[pl.BlockSpec((tr, D_MODEL), lambda i: (i, 0)),
                  _row_spec(l, D_MODEL, 1), _row_spec(l, D_MODEL, 1)],
        out_specs=[pl.BlockSpec((tr, D_MODEL), lambda i: (jnp.minimum(i, n_prompt - 1), 0)),
                   pl.BlockSpec((tr, D_MODEL), lambda i: (0, 0))],
        out_shape=[jax.ShapeDtypeStruct((m_prompt, D_MODEL), F32),
                   jax.ShapeDtypeStruct((tr, D_MODEL), F32)],
        compiler_params=_params("arbitrary"),
        name="ln_rows_split",
    )(z, _row_param(ln_g), _row_param(ln_b))


def _sample_mix_kernel(h_ref, ha_ref, hb_ref, wa_ref, wb_ref, bias_ref, lbg_ref, lbb_ref,
                       ldg_ref, ldb_ref, ws0_ref, bs0_ref, cos_ref, sin_ref,
                       ya_in, yb_in, yd_in,
                       ya_ref, yb_ref, yd_ref, na_ref, nb_ref, vn_ref, qk_ref):
    del ya_in, yb_in, yd_in
    col = lambda c, w: h_ref[:, c:c + w]
    u_a = col(COL_A_C, W_A) * col(COL_A_H, W_A)
    conv_a = wa_ref[0:1, :] * ha_ref[0] + wa_ref[1:2, :] * ha_ref[1] + wa_ref[2:3, :] * u_a
    ya_ref[...] = (col(COL_A_B, W_A) * conv_a * _silu(col(COL_A_Z, W_A))).astype(BF16)
    na_ref[0] = ha_ref[1]
    na_ref[1] = u_a
    u_b = col(COL_B_A, W_B) * jax.nn.sigmoid(col(COL_B_G, W_B))
    cb = wb_ref[CONV_B - 1:CONV_B, :] * u_b + bias_ref[...]
    for k in range(CONV_B - 1):
        cb = cb + wb_ref[k:k + 1, :] * hb_ref[k]
    ln_b = _layernorm(cb, lbg_ref[...], lbb_ref[...])
    yb_ref[...] = (_silu(ln_b) * _silu(col(COL_B_Z, W_B))).astype(BF16)
    for k in range(CONV_B - 2):
        nb_ref[k] = hb_ref[k + 1]
    nb_ref[CONV_B - 2] = u_b
    vn = _layernorm(col(COL_D_V, W_D), ldg_ref[...], ldb_ref[...])
    vn_ref[...] = vn
    mix = ws0_ref[...] * vn + bs0_ref[...]
    yd_ref[...] = (col(COL_D_U, W_D) * mix * _silu(col(COL_D_Z, W_D))).astype(BF16)
    cos = cos_ref[...]
    sin = sin_ref[...]
    for hh in range(2 * N_HEADS):
        cs = slice(hh * HEAD_DIM, (hh + 1) * HEAD_DIM)
        qk_ref[:, cs] = _rope(h_ref[:, COL_Q + hh * HEAD_DIM:COL_Q + (hh + 1) * HEAD_DIM], cos, sin)


def _sample_mix(h, m_prompt, ys, hist_a, hist_b, conv_a_w, conv_b_w, conv_b_bias, ln_b_g, ln_b_b,
                ln_d_g, ln_d_b, w_s, b_s, cos, sin, l):
    r = SAMPLE_BLOCK
    blk = m_prompt // r
    ws0 = jnp.repeat(w_s[l, :, 0, 0], LANES)[None, :]
    bs0 = jnp.repeat(b_s[l, :, 0], LANES)[None, :]
    row = lambda p: p[l][None, :]
    small = [hist_a, hist_b, conv_a_w[l], conv_b_w[l], row(conv_b_bias), row(ln_b_g), row(ln_b_b),
             row(ln_d_g), row(ln_d_b), ws0, bs0, cos, sin]
    whole = lambda a: pl.BlockSpec(a.shape, lambda i, nd=a.ndim: (0,) * nd)
    y_spec = lambda w: pl.BlockSpec((r, w), lambda i: (blk, 0))
    f = lambda shape: jax.ShapeDtypeStruct(shape, F32)
    n_in = 1 + len(small)
    return pl.pallas_call(
        _sample_mix_kernel,
        grid=(1,),
        in_specs=([pl.BlockSpec((r, N_BRANCH_COLS), lambda i: (blk, 0))]
                  + [whole(a) for a in small]
                  + [pl.BlockSpec(memory_space=pl.ANY)] * 3),
        out_specs=[y_spec(W_A), y_spec(W_B), y_spec(W_D),
                   whole(f((CONV_A - 1, r, W_A))), whole(f((CONV_B - 1, r, W_B))),
                   whole(f((r, W_D))), whole(f((r, 2 * W_QKV)))],
        out_shape=[jax.ShapeDtypeStruct(ys[0].shape, BF16), jax.ShapeDtypeStruct(ys[1].shape, BF16),
                   jax.ShapeDtypeStruct(ys[2].shape, BF16),
                   f((CONV_A - 1, r, W_A)), f((CONV_B - 1, r, W_B)), f((r, W_D)), f((r, 2 * W_QKV))],
        input_output_aliases={n_in: 0, n_in + 1: 1, n_in + 2: 2},
        compiler_params=_params("arbitrary"),
        name="sample_mix",
    )(h, *small, *ys)


def _sample_attn_kernel(qk_ref, v_ref, cz_ref, c0_ref, c1_ref, c2_ref, yc_in, y_ref,
                        acc_ref, yrow_ref):
    del yc_in
    b = pl.program_id(0)
    scale = HEAD_DIM ** -0.5

    @pl.when(b == 0)
    def _():
        acc_ref[...] = jnp.zeros(acc_ref.shape, F32)

    os_, ls_ = [], []
    for g, c_ref in enumerate((c0_ref, c1_ref, c2_ref)):
        q = qk_ref[b, pl.ds(g * HEADS_PER_GROUP, HEADS_PER_GROUP), :]
        k_new = qk_ref[b, pl.ds(N_HEADS + g * HEADS_PER_GROUP, HEADS_PER_GROUP), :]
        v_new = v_ref[b, pl.ds(g * HEADS_PER_GROUP, HEADS_PER_GROUP), :]
        kc = c_ref[:, 0]
        vc = c_ref[:, 1]
        s_c = jnp.sum(kc * q[None], axis=-1, keepdims=True) * scale
        s_n = jnp.sum(k_new * q, axis=-1, keepdims=True) * scale
        mx = jnp.maximum(jnp.max(s_c, axis=0), s_n)
        p_c = jnp.exp(s_c - mx[None])
        p_n = jnp.exp(s_n - mx)
        den = jnp.sum(p_c, axis=0) + p_n
        os_.append((jnp.sum(p_c * vc, axis=0) + p_n * v_new) / den)
        ls_.append(mx + jnp.log(den))
    yrow_ref[...] = _merge_groups(os_, ls_) * _silu(cz_ref[b])
    is_b = lax.broadcasted_iota(jnp.int32, (acc_ref.shape[0], HEAD_DIM), 0) == b
    for hs in range(HEADS_PER_GROUP):
        cs = slice(hs * HEAD_DIM, (hs + 1) * HEAD_DIM)
        acc_ref[:, cs] = jnp.where(is_b, jnp.broadcast_to(yrow_ref[hs:hs + 1, :], is_b.shape),
                                   acc_ref[:, cs])

    @pl.when(b == pl.num_programs(0) - 1)
    def _():
        y_ref[...] = acc_ref[...].astype(BF16)


def _sample_attn(qk, h_s, caches, y_c, m_prompt, l, nb):
    r = SAMPLE_BLOCK
    qk3 = qk.reshape(r, 2 * N_HEADS, HEAD_DIM)
    v3 = h_s[:, COL_V:COL_V + W_QKV].reshape(r, N_HEADS, HEAD_DIM)
    cz3 = h_s[:, COL_C_Z:COL_C_Z + W_C].reshape(r, HEADS_PER_GROUP, HEAD_DIM)
    whole = lambda a: pl.BlockSpec(a.shape, lambda b, nd=a.ndim: (0,) * nd)
    cspec = pl.BlockSpec((None, None, SPAN, None, 2, HEADS_PER_GROUP, HEAD_DIM),
                         lambda b: (l, b, 0, 0, 0, 0, 0))
    return pl.pallas_call(
        _sample_attn_kernel,
        grid=(nb,),
        in_specs=[whole(qk3), whole(v3), whole(cz3), cspec, cspec, cspec,
                  pl.BlockSpec(memory_space=pl.ANY)],
        out_specs=pl.BlockSpec((r, W_C), lambda b: (m_prompt // r, 0)),
        out_shape=jax.ShapeDtypeStruct(y_c.shape, BF16),
        scratch_shapes=[pltpu.VMEM((r, W_C), F32), pltpu.VMEM((HEADS_PER_GROUP, HEAD_DIM), F32)],
        input_output_aliases={6: 0},
        compiler_params=_params("arbitrary"),
        name="sample_attn",
    )(qk3, v3, cz3, *caches, y_c)


def _kv_rows(k_rot, h, nb, s, g):
    keep = k_rot.shape[1]
    c0 = COL_V + g * W_C
    v = h[:nb * s, c0:c0 + W_C].reshape(nb, s, W_C)[:, -keep:]
    return jnp.stack([k_rot, v], axis=2).reshape(nb, keep, 2, HEADS_PER_GROUP, HEAD_DIM)


def _cache_view(c, dil):
    assert c.shape[2] == SPAN * dil
    return c.reshape(c.shape[0], c.shape[1], SPAN, dil, 2, HEADS_PER_GROUP, HEAD_DIM)


def kernel(x_prompt, x_sample, state_conv_a, state_conv_b, cache_kv_w128, cache_kv_w512,
           cache_kv_w2048, w_in, conv_a_w, conv_b_w, conv_b_bias, ln_b_g, ln_b_b, ln_d_g,
           ln_d_b, w_s, b_s, w_out_a, w_out_b, w_out_c, w_out_d, b_gate, w_o, ln_g, ln_b):
    bp, tp, _ = x_prompt.shape
    bs, ts, _ = x_sample.shape
    assert ts == 1 and bs <= SAMPLE_BLOCK
    mp = bp * tp
    assert mp % SAMPLE_BLOCK == 0
    m_all = mp + SAMPLE_BLOCK
    pad = SAMPLE_BLOCK - bs
    tiles = _tiles(m_all)

    w_gate_b = w_in[:, :, COL_GATE:].astype(BF16)
    w_o_b = w_o.astype(BF16)
    wouts_b = [w.astype(BF16) for w in (w_out_a, w_out_b, w_out_c, w_out_d)]
    caches = [_cache_view(c, d)
              for c, d in zip((cache_kv_w128, cache_kv_w512, cache_kv_w2048), DILATIONS)]

    cos_p, sin_p = _rope_tables(jnp.arange(tp, dtype=jnp.int32))
    cos_s, sin_s = _rope_tables(PAST_LEN + jnp.arange(ts, dtype=jnp.int32))

    x = jnp.concatenate([x_prompt.reshape(mp, D_MODEL), x_sample.reshape(bs, D_MODEL),
                         jnp.zeros((pad, D_MODEL), F32)], axis=0)
    xb = x.astype(BF16)

    ca_p, ca_s, cb_p, cb_s, dv_s = [], [], [], [], []
    kv_p = [[] for _ in range(N_GROUPS)]
    kv_s = [[] for _ in range(N_GROUPS)]
    for l in range(DEPTH):
        h = _in_proj(xb, w_in, l, tiles["in_proj"], 512)
        y_a, st_a = _branch_a(h, bp, tp, conv_a_w, l, 512)
        y_b, st_b = _branch_b(h, bp, tp, conv_b_w, conv_b_bias, ln_b_g, ln_b_b, l, 256)
        y_d = _branch_d(h, bp, tp, ln_d_g, ln_d_b, w_s, b_s, l, 512)
        o0, l0, kr0 = _attn_group(h, bp, tp, cos_p, sin_p, 0, 8)
        o1, l1, kr1 = _attn_group(h, bp, tp, cos_p, sin_p, 1, 2)
        y_c, kr2 = _attn_group(h, bp, tp, cos_p, sin_p, 2, 1, others=(o0, l0, o1, l1))
        ca_p.append(st_a)
        cb_p.append(st_b)
        for g, kr in enumerate((kr0, kr1, kr2)):
            kv_p[g].append(_kv_rows(kr, h, bp, tp, g))

        hist_a = jnp.pad(state_conv_a[l].transpose(1, 0, 2), ((0, 0), (0, pad), (0, 0)))
        hist_b = jnp.pad(state_conv_b[l].transpose(1, 0, 2), ((0, 0), (0, pad), (0, 0)))
        y_a, y_b, y_d, na, nb_, vn, qk_s = _sample_mix(
            h, mp, (y_a, y_b, y_d), hist_a, hist_b, conv_a_w, conv_b_w, conv_b_bias,
            ln_b_g, ln_b_b, ln_d_g, ln_d_b, w_s, b_s, cos_s, sin_s, l)
        h_s = h[mp:]
        y_c = _sample_attn(qk_s, h_s, caches, y_c, mp, l, bs)
        ca_s.append(na[:, :bs].transpose(1, 0, 2))
        cb_s.append(nb_[:, :bs].transpose(1, 0, 2))
        dv_s.append(vn[:bs].reshape(bs, ts, W_D))
        for g in range(N_GROUPS):
            c0 = g * W_C
            k = qk_s[:bs, W_QKV + c0:W_QKV + c0 + W_C]
            v = h_s[:bs, COL_V + c0:COL_V + c0 + W_C]
            kv_s[g].append(jnp.stack([k, v], axis=1).reshape(bs, ts, 2, HEADS_PER_GROUP, HEAD_DIM))

        merged = _gate_out(xb, w_gate_b, b_gate, (y_a, y_b, y_c, y_d), wouts_b, l,
                           tiles["gate_out"], 256)
        z = _proj_res(merged, w_o_b, x, l, tiles["proj_res"], 512)
        if l + 1 < DEPTH:
            x, xb = _ln_rows(z, ln_g, ln_b, l, tiles["ln_rows"])
        else:
            y_p, y_s = _ln_rows_split(z, ln_g, ln_b, l, mp)

    return (y_p.reshape(bp, tp, D_MODEL), y_s[:bs].reshape(bs, ts, D_MODEL),
            jnp.stack(ca_p), jnp.stack(ca_s), jnp.stack(cb_p), jnp.stack(cb_s),
            jnp.stack(kv_p[0]), jnp.stack(kv_s[0]), jnp.stack(kv_p[1]), jnp.stack(kv_s[1]),
            jnp.stack(kv_p[2]), jnp.stack(kv_s[2]), jnp.stack(dv_s))
```

```python
import functools

import jax
import jax.numpy as jnp
from jax import lax
from jax.experimental import pallas as pl
from jax.experimental.pallas import tpu as pltpu

D_MODEL = 4096
DEPTH = 2
PAST_LEN = 16384
W_A = 1024
W_B = 1024
HEAD_DIM = 128
N_GROUPS = 3
HEADS_PER_GROUP = 4
N_HEADS = N_GROUPS * HEADS_PER_GROUP
W_QKV = N_HEADS * HEAD_DIM
W_C = HEADS_PER_GROUP * HEAD_DIM
W_D = 1024
CHUNK = 128
D_GROUPS = 8
CONV_A = 3
CONV_B = 31
WINDOWS = (128, 512, 2048)
DILATIONS = (1, 4, 16)
SPAN = 128
ROPE_THETA = 10000.0
N_BRANCH = 4
LN_EPS = 1e-5
ALPHA = (2 * DEPTH) ** 0.25

COL_A_B, COL_A_C, COL_A_H, COL_A_Z = 0, 1024, 2048, 3072
COL_B_A, COL_B_G, COL_B_Z = 4096, 5120, 6144
COL_Q, COL_K, COL_V, COL_C_Z = 7168, 8704, 10240, 11776
COL_D_U, COL_D_V, COL_D_Z = 12288, 13312, 14336
N_BRANCH_COLS = 15360
COL_GATE = 15360

LANES = 128
SUBLANES = 8
SAMPLE_BLOCK = 64
VMEM_LIMIT_BYTES = 56 * 1024 * 1024

F32 = jnp.float32
BF16 = jnp.bfloat16


def _params(*sem):
    return pltpu.CompilerParams(dimension_semantics=sem, vmem_limit_bytes=VMEM_LIMIT_BYTES)


def _silu(v):
    return v * jax.nn.sigmoid(v)


def _layernorm(v, g, b):
    mu = jnp.mean(v, axis=-1, keepdims=True)
    c = v - mu
    var = jnp.mean(c * c, axis=-1, keepdims=True)
    return c * lax.rsqrt(var + LN_EPS) * g + b


def _row_param(p):
    return p.reshape(DEPTH, 1, p.shape[-1])


def _row_spec(l, c, nargs):
    if nargs == 1:
        return pl.BlockSpec((None, 1, c), lambda b: (l, 0, 0))
    return pl.BlockSpec((None, 1, c), lambda b, t: (l, 0, 0))


def _tiles(m_all):
    def largest(limit, mult):
        return max(t for t in range(mult, limit + 1, mult) if m_all % t == 0)
    m_prompt = m_all - SAMPLE_BLOCK
    return dict(in_proj=largest(2176, 16), gate_out=largest(768, 16),
                proj_res=largest(1536, 16), ln_rows=largest(384, 16),
                ln_out=max(t for t in (512, 256, 128, 64) if m_prompt % t == 0))


def _mm_kernel(x_ref, w_ref, o_ref):
    o_ref[...] = jnp.dot(x_ref[...], w_ref[...].astype(BF16), preferred_element_type=F32)


def _in_proj(xb, w_in, l, tm, tn):
    m = xb.shape[0]
    return pl.pallas_call(
        _mm_kernel,
        grid=(m // tm, N_BRANCH_COLS // tn),
        in_specs=[pl.BlockSpec((tm, D_MODEL), lambda i, j: (i, 0), pipeline_mode=pl.Buffered(1)),
                  pl.BlockSpec((None, D_MODEL, tn), lambda i, j: (l, 0, j))],
        out_specs=pl.BlockSpec((tm, tn), lambda i, j: (i, j)),
        out_shape=jax.ShapeDtypeStruct((m, N_BRANCH_COLS), F32),
        compiler_params=_params("parallel", "parallel"),
        name="in_proj",
    )(xb, w_in)


A_PAD = 8


def _branch_a_kernel(ab_ref, ac_ref, ah_ref, az_ref, w_ref, y_ref, st_ref, ext_ref):
    t = pl.program_id(1)
    tt = ab_ref.shape[0]

    @pl.when(t == 0)
    def _():
        ext_ref[0:A_PAD, :] = jnp.zeros((A_PAD, W_A), F32)

    @pl.when(t > 0)
    def _():
        ext_ref[0:A_PAD, :] = ext_ref[tt:tt + A_PAD, :]

    u = ac_ref[...] * ah_ref[...]
    ext_ref[A_PAD:A_PAD + tt, :] = u
    w = w_ref[...]
    conv = (w[0:1, :] * ext_ref[A_PAD - 2:A_PAD - 2 + tt, :]
            + w[1:2, :] * ext_ref[A_PAD - 1:A_PAD - 1 + tt, :]
            + w[2:3, :] * u)
    y_ref[...] = (ab_ref[...] * conv * _silu(az_ref[...])).astype(BF16)
    st_ref[...] = ext_ref[A_PAD + tt - (CONV_A - 1):A_PAD + tt, :]


def _branch_a(h, nb, s, conv_a_w, l, tt):
    nt = s // tt
    col = lambda c: pl.BlockSpec((tt, W_A), lambda b, t: (b * nt + t, c // W_A))
    return pl.pallas_call(
        _branch_a_kernel,
        grid=(nb, nt),
        in_specs=[col(COL_A_B), col(COL_A_C), col(COL_A_H), col(COL_A_Z),
                  pl.BlockSpec((None, CONV_A, W_A), lambda b, t: (l, 0, 0))],
        out_specs=[pl.BlockSpec((tt, W_A), lambda b, t: (b * nt + t, 0)),
                   pl.BlockSpec((None, CONV_A - 1, W_A), lambda b, t: (b, 0, 0))],
        out_shape=[jax.ShapeDtypeStruct((h.shape[0], W_A), BF16),
                   jax.ShapeDtypeStruct((nb, CONV_A - 1, W_A), F32)],
        scratch_shapes=[pltpu.VMEM((A_PAD + tt, W_A), F32)],
        compiler_params=_params("parallel", "arbitrary"),
        name="branch_a",
    )(h, h, h, h, conv_a_w)


B_PAD = 32
B_ROWS = 64


def _branch_b_kernel(ba_ref, bg_ref, bz_ref, w_ref, bias_ref, g_ref, b_ref, wg_ref,
                     y_ref, st_ref, wgb_ref, ext_ref, sh_ref, cb_ref):
    t = pl.program_id(1)
    tt = ba_ref.shape[0]
    wgb_ref[...] = wg_ref[...].astype(BF16)

    @pl.when(t == 0)
    def _():
        ext_ref[0:B_PAD, :] = jnp.zeros((B_PAD, W_B), F32)

    @pl.when(t > 0)
    def _():
        ext_ref[0:B_PAD, :] = ext_ref[tt:tt + B_PAD, :]

    ext_ref[B_PAD:B_PAD + tt, :] = ba_ref[...] * jax.nn.sigmoid(bg_ref[...])
    sh_rows = sh_ref.shape[1]
    for s in range(1, SUBLANES):
        sh_ref[s - 1] = ext_ref[s:s + sh_rows, :]
    first = B_PAD - (CONV_B - 1)
    for c in range(W_B // LANES):
        cs = slice(c * LANES, (c + 1) * LANES)
        wc = w_ref[:, cs]
        for rc in range(tt // B_ROWS):
            acc = jnp.zeros((B_ROWS, LANES), F32)
            for k in range(CONV_B):
                q, s = divmod(first + k, SUBLANES)
                r0 = SUBLANES * q + rc * B_ROWS
                if s == 0:
                    tap = ext_ref[r0:r0 + B_ROWS, cs]
                else:
                    tap = sh_ref[s - 1, r0:r0 + B_ROWS, cs]
                acc = acc + wc[k:k + 1, :] * tap
            cb_ref[rc * B_ROWS:(rc + 1) * B_ROWS, cs] = acc
    cb = cb_ref[...] + bias_ref[...]
    ln = _layernorm(cb, g_ref[...], b_ref[...])
    y_ref[...] = (_silu(ln) * _silu(bz_ref[...])).astype(BF16)
    st_ref[...] = ext_ref[B_PAD + tt - (CONV_B - 1):B_PAD + tt, :]


def _branch_b(h, nb, s, conv_b_w, conv_b_bias, ln_b_g, ln_b_b, w_in, l, tt):
    nt = s // tt
    n_gate = N_BRANCH * D_MODEL
    wcols = n_gate // (nb * nt)
    assert wcols % LANES == 0 and COL_GATE % wcols == 0
    col = lambda c: pl.BlockSpec((tt, W_B), lambda b, t: (b * nt + t, c // W_B))
    return pl.pallas_call(
        _branch_b_kernel,
        grid=(nb, nt),
        in_specs=[col(COL_B_A), col(COL_B_G), col(COL_B_Z),
                  pl.BlockSpec((None, CONV_B, W_B), lambda b, t: (l, 0, 0)),
                  _row_spec(l, W_B, 2), _row_spec(l, W_B, 2), _row_spec(l, W_B, 2),
                  pl.BlockSpec((None, D_MODEL, wcols),
                               lambda b, t: (l, 0, COL_GATE // wcols + b * nt + t))],
        out_specs=[pl.BlockSpec((tt, W_B), lambda b, t: (b * nt + t, 0)),
                   pl.BlockSpec((None, CONV_B - 1, W_B), lambda b, t: (b, 0, 0)),
                   pl.BlockSpec((D_MODEL, wcols), lambda b, t: (0, b * nt + t))],
        out_shape=[jax.ShapeDtypeStruct((h.shape[0], W_B), BF16),
                   jax.ShapeDtypeStruct((nb, CONV_B - 1, W_B), F32),
                   jax.ShapeDtypeStruct((D_MODEL, n_gate), BF16)],
        scratch_shapes=[pltpu.VMEM((B_PAD + tt, W_B), F32),
                        pltpu.VMEM((SUBLANES - 1, B_PAD + tt - SUBLANES, W_B), F32),
                        pltpu.VMEM((tt, W_B), F32)],
        compiler_params=_params("parallel", "arbitrary"),
        name="branch_b",
    )(h, h, h, conv_b_w, _row_param(conv_b_bias), _row_param(ln_b_g), _row_param(ln_b_b), w_in)


def _branch_d_kernel(du_ref, dv_ref, dz_ref, g_ref, b_ref, ws_ref, bs_ref,
                     wa_ref, wb_ref, wc_ref, wd_ref, y_ref, wab_ref, wbb_ref, wcb_ref, wdb_ref):
    tt = du_ref.shape[0]
    for src, dst in ((wa_ref, wab_ref), (wb_ref, wbb_ref), (wc_ref, wcb_ref), (wd_ref, wdb_ref)):
        dst[...] = src[...].astype(BF16)
    vb = _layernorm(dv_ref[...], g_ref[...], b_ref[...]).astype(BF16)
    row = lax.broadcasted_iota(jnp.int32, (CHUNK, CHUNK), 0)
    col = lax.broadcasted_iota(jnp.int32, (CHUNK, CHUNK), 1)
    causal = col <= row
    for g in range(D_GROUPS):
        cs = slice(g * LANES, (g + 1) * LANES)
        wm = jnp.where(causal, ws_ref[g], 0.0).astype(BF16)
        for c in range(tt // CHUNK):
            rs = slice(c * CHUNK, (c + 1) * CHUNK)
            mix = jnp.dot(wm, vb[rs, cs], preferred_element_type=F32) + bs_ref[:, cs]
            y_ref[rs, cs] = (du_ref[rs, cs] * mix * _silu(dz_ref[rs, cs])).astype(BF16)


def _bias_slab(b_s_l):
    return jnp.repeat(b_s_l.T, LANES, axis=1)


def _branch_d(h, nb, s, ln_d_g, ln_d_b, w_s, b_s, wouts, l, tt):
    nt = s // tt
    steps = nb * nt
    col = lambda c: pl.BlockSpec((tt, W_D), lambda b, t: (b * nt + t, c // W_D))
    w_in_specs, w_out_specs, w_out_shapes = [], [], []
    for w in wouts:
        rows = w.shape[1] // steps
        assert rows % 16 == 0 and rows * steps == w.shape[1]
        w_in_specs.append(pl.BlockSpec((None, rows, D_MODEL), lambda b, t: (l, b * nt + t, 0)))
        w_out_specs.append(pl.BlockSpec((rows, D_MODEL), lambda b, t: (b * nt + t, 0)))
        w_out_shapes.append(jax.ShapeDtypeStruct((w.shape[1], D_MODEL), BF16))
    return pl.pallas_call(
        _branch_d_kernel,
        grid=(nb, nt),
        in_specs=[col(COL_D_U), col(COL_D_V), col(COL_D_Z),
                  _row_spec(l, W_D, 2), _row_spec(l, W_D, 2),
                  pl.BlockSpec((None, D_GROUPS, CHUNK, CHUNK), lambda b, t: (l, 0, 0, 0)),
                  pl.BlockSpec((CHUNK, W_D), lambda b, t: (0, 0))] + w_in_specs,
        out_specs=[pl.BlockSpec((tt, W_D), lambda b, t: (b * nt + t, 0))] + w_out_specs,
        out_shape=[jax.ShapeDtypeStruct((h.shape[0], W_D), BF16)] + w_out_shapes,
        compiler_params=_params("parallel", "parallel"),
        name="branch_d",
    )(h, h, h, _row_param(ln_d_g), _row_param(ln_d_b), w_s, _bias_slab(b_s[l]), *wouts)


def _rope_tables(pos):
    half = HEAD_DIM // 2
    inv = ROPE_THETA ** (-jnp.arange(half, dtype=F32) / half)
    ang = pos.astype(F32)[:, None] * inv[None, :]
    cos = jnp.cos(ang)
    sin = jnp.sin(ang)
    return jnp.concatenate([cos, cos], axis=-1), jnp.concatenate([-sin, sin], axis=-1)


def _rope(v, cos, sin):
    return v * cos + pltpu.roll(v, HEAD_DIM // 2, v.ndim - 1) * sin


def _merge_groups(os_, ls_):
    mx = jnp.maximum(jnp.maximum(ls_[0], ls_[1]), ls_[2])
    es = [jnp.exp(l_ - mx) for l_ in ls_]
    den = es[0] + es[1] + es[2]
    return (es[0] / den) * os_[0] + (es[1] / den) * os_[1] + (es[2] / den) * os_[2]


ATTN_ILP = 8


def _attn_kernel(*refs, dil, blocks, use_prev, keep, merge):
    refs = list(refs)
    q_ref, kc_ref, vc_ref, cos_ref, sin_ref = refs[:5]
    del refs[:5]
    if use_prev:
        kp_ref, vp_ref, cosp_ref, sinp_ref = refs[:4]
        del refs[:4]
    if merge:
        oa_ref, la_ref, ob_ref, lb_ref, cz_ref, y_ref = refs[:6]
        del refs[:6]
    else:
        o_ref, l_ref = refs[:2]
        del refs[:2]
    kr_ref, qs_ref, ks_ref = refs[:3]
    del refs[:3]
    if use_prev:
        kps_ref = refs.pop(0)
    if merge:
        ys_ref = refs.pop(0)

    n = pl.program_id(2)
    tile = q_ref.shape[0]
    scale = HEAD_DIM ** -0.5
    cos = cos_ref[...]
    sin = sin_ref[...]
    qs_ref[...] = _rope(q_ref[...], cos, sin)
    ks_ref[...] = _rope(kc_ref[...], cos, sin)
    kr_ref[...] = ks_ref[tile - keep:tile, :]
    if use_prev:
        kps_ref[...] = _rope(kp_ref[...], cosp_ref[...], sinp_ref[...])

    width = 2 * SPAN if use_prev else SPAN
    row = lax.broadcasted_iota(jnp.int32, (SPAN, width), 0)
    col = lax.broadcasted_iota(jnp.int32, (SPAN, width), 1)
    if use_prev:
        band = jnp.logical_or(jnp.logical_and(col < SPAN, col >= row),
                              jnp.logical_and(col >= SPAN, col - SPAN <= row))
        has_prev = jnp.zeros((SPAN, width), jnp.int32) + n > 0
        first = jnp.logical_and(band, jnp.logical_or(col >= SPAN, has_prev))
    else:
        band = col <= row
        first = band
    nt = (((1,), (1,)), ((), ()))

    def rows(j, r, count=SPAN):
        start = j * SPAN * dil + r
        return pl.ds(start, count) if dil == 1 else pl.ds(start, count, stride=dil)

    def load_kv(j, r):
        if not use_prev:
            return ks_ref[rows(j, r), :].astype(BF16), vc_ref[rows(j, r), :].astype(BF16)
        if j > 0:
            both = rows(j - 1, r, 2 * SPAN)
            return ks_ref[both, :].astype(BF16), vc_ref[both, :].astype(BF16)
        k2 = jnp.concatenate([kps_ref[rows(0, r), :], ks_ref[rows(0, r), :]], axis=0)
        v2 = jnp.concatenate([vp_ref[rows(0, r), :], vc_ref[rows(0, r), :]], axis=0)
        return k2.astype(BF16), v2.astype(BF16)

    def run(grp):
        qs = [qs_ref[rows(j, r), :].astype(BF16) for r, j in grp]
        kvs = [load_kv(j, r) for r, j in grp]
        ss = [lax.dot_general(q, k2, nt, preferred_element_type=F32) * scale
              for q, (k2, _) in zip(qs, kvs)]
        ss = [jnp.where(first if j == 0 else band, s, -jnp.inf) for s, (r, j) in zip(ss, grp)]
        mxs = [jnp.max(s, axis=1, keepdims=True) for s in ss]
        ps = [jnp.exp(s - mx) for s, mx in zip(ss, mxs)]
        dens = [jnp.sum(p, axis=1, keepdims=True) for p in ps]
        outs = [jnp.dot((p / den).astype(BF16), v2, preferred_element_type=F32)
                for p, den, (_, v2) in zip(ps, dens, kvs)]
        for (r, j), o, mx, den in zip(grp, outs, mxs, dens):
            idx = rows(j, r)
            lse = jnp.broadcast_to(mx + jnp.log(den), (SPAN, HEAD_DIM))
            if merge:
                om = _merge_groups([oa_ref[idx, :], ob_ref[idx, :], o],
                                   [la_ref[idx, :], lb_ref[idx, :], lse])
                ys_ref[idx, :] = om * _silu(cz_ref[idx, :])
            else:
                o_ref[idx, :] = o
                l_ref[idx, :] = lse

    order = [(r, j) for r in range(dil) for j in range(blocks)]
    for i in range(0, len(order), ATTN_ILP):
        run(order[i:i + ATTN_ILP])
    if merge:
        y_ref[...] = ys_ref[...].astype(BF16)


def _attn_group(h, nb, s, cos, sin, g, blocks, others=None):
    dil = DILATIONS[g]
    unit = SPAN * dil
    tile = unit * blocks
    use_prev = s > unit
    per = tile // unit
    nt, nu = s // tile, s // unit
    keep = min(WINDOWS[g], s)
    assert keep <= tile and s % tile == 0
    merge = others is not None
    head = lambda c0: (lambda hs: c0 // HEAD_DIM + g * HEADS_PER_GROUP + hs)
    cq, ck, cv = head(COL_Q), head(COL_K), head(COL_V)
    prev = lambda n: jnp.maximum(n * per - 1, 0)
    col_blk = lambda rows_, fn: pl.BlockSpec((rows_, HEAD_DIM), fn)
    own = lambda c: col_blk(tile, lambda b, hs, n: (b * nt + n, c(hs)))
    in_specs = [own(cq), own(ck), own(cv),
                col_blk(tile, lambda b, hs, n: (n, 0)), col_blk(tile, lambda b, hs, n: (n, 0))]
    args = [h, h, h, cos, sin]
    if use_prev:
        before = lambda c: col_blk(unit, lambda b, hs, n: (b * nu + prev(n), c(hs)))
        in_specs += [before(ck), before(cv),
                     col_blk(unit, lambda b, hs, n: (prev(n), 0)),
                     col_blk(unit, lambda b, hs, n: (prev(n), 0))]
        args += [h, h, cos, sin]
    plain = col_blk(tile, lambda b, hs, n: (b * nt + n, hs))
    if merge:
        in_specs += [plain] * 4 + [own(lambda hs: COL_C_Z // HEAD_DIM + hs)]
        args += list(others) + [h]
        out_specs = [plain]
        out_shape = [jax.ShapeDtypeStruct((h.shape[0], W_C), BF16)]
    else:
        out_specs = [plain, plain]
        out_shape = [jax.ShapeDtypeStruct((nb * s, W_C), F32)] * 2
    out_specs.append(pl.BlockSpec((None, keep, HEAD_DIM), lambda b, hs, n: (b, 0, hs)))
    out_shape.append(jax.ShapeDtypeStruct((nb, keep, W_C), F32))
    scratch = [pltpu.VMEM((tile, HEAD_DIM), F32), pltpu.VMEM((tile, HEAD_DIM), F32)]
    if use_prev:
        scratch.append(pltpu.VMEM((unit, HEAD_DIM), F32))
    if merge:
        scratch.append(pltpu.VMEM((tile, HEAD_DIM), F32))
    kern = functools.partial(_attn_kernel, dil=dil, blocks=blocks, use_prev=use_prev,
                             keep=keep, merge=merge)
    return pl.pallas_call(
        kern,
        grid=(nb, HEADS_PER_GROUP, nt),
        in_specs=in_specs,
        out_specs=out_specs,
        out_shape=out_shape,
        scratch_shapes=scratch,
        compiler_params=_params("parallel", "parallel", "arbitrary"),
        name=f"attn_g{g}",
    )(*args)


def _gate_out_kernel(x_ref, wg0, wg1, wg2, wg3, bg0, bg1, bg2, bg3,
                     ya, yb, yc, yd, woa, wob, woc, wod, o_ref):
    x = x_ref[...]
    acc = None
    for wg, bg, y, wo in ((wg0, bg0, ya, woa), (wg1, bg1, yb, wob),
                          (wg2, bg2, yc, woc), (wg3, bg3, yd, wod)):
        gate = jax.nn.sigmoid(jnp.dot(x, wg[...], preferred_element_type=F32) + bg[...])
        part = jnp.dot(y[...], wo[...], preferred_element_type=F32)
        acc = gate * part if acc is None else acc + gate * part
    o_ref[...] = acc.astype(BF16)


def _gate_out(xb, w_gate_b, b_gate, ys, wouts_b, l, tm, tn):
    m = xb.shape[0]
    nj = D_MODEL // tn
    wg_spec = lambda br: pl.BlockSpec((D_MODEL, tn), lambda i, j: (0, br * nj + j))
    bg_spec = lambda br: pl.BlockSpec((None, 1, tn), lambda i, j: (l, 0, br * nj + j))
    y_spec = lambda y: pl.BlockSpec((tm, y.shape[1]), lambda i, j: (i, 0))
    wo_spec = lambda w: pl.BlockSpec((w.shape[0], tn), lambda i, j: (0, j))
    bg3 = b_gate.reshape(DEPTH, 1, N_BRANCH * D_MODEL)
    return pl.pallas_call(
        _gate_out_kernel,
        grid=(m // tm, nj),
        in_specs=([pl.BlockSpec((tm, D_MODEL), lambda i, j: (i, 0))]
                  + [wg_spec(br) for br in range(N_BRANCH)]
                  + [bg_spec(br) for br in range(N_BRANCH)]
                  + [y_spec(y) for y in ys]
                  + [wo_spec(w) for w in wouts_b]),
        out_specs=pl.BlockSpec((tm, tn), lambda i, j: (i, j)),
        out_shape=jax.ShapeDtypeStruct((m, D_MODEL), BF16),
        compiler_params=_params("parallel", "parallel"),
        name="gate_out",
    )(xb, *([w_gate_b] * N_BRANCH), *([bg3] * N_BRANCH), *ys, *wouts_b)


def _proj_res_kernel(m_ref, wo_ref, x_ref, z_ref):
    z_ref[...] = ALPHA * x_ref[...] + jnp.dot(m_ref[...], wo_ref[...], preferred_element_type=F32)


def _proj_res(merged, w_o_b, x, l, tm, tn):
    m = x.shape[0]
    return pl.pallas_call(
        _proj_res_kernel,
        grid=(m // tm, D_MODEL // tn),
        in_specs=[pl.BlockSpec((tm, D_MODEL), lambda i, j: (i, 0)),
                  pl.BlockSpec((None, D_MODEL, tn), lambda i, j: (l, 0, j)),
                  pl.BlockSpec((tm, tn), lambda i, j: (i, j))],
        out_specs=pl.BlockSpec((tm, tn), lambda i, j: (i, j)),
        out_shape=jax.ShapeDtypeStruct((m, D_MODEL), F32),
        compiler_params=_params("parallel", "parallel"),
        name="proj_res",
    )(merged, w_o_b, x)


def _ln_rows_kernel(z_ref, g_ref, b_ref, y_ref, yb_ref):
    y = _layernorm(z_ref[...], g_ref[...], b_ref[...])
    y_ref[...] = y
    yb_ref[...] = y.astype(BF16)


def _ln_rows(z, ln_g, ln_b, l, tr):
    m = z.shape[0]
    blk = pl.BlockSpec((tr, D_MODEL), lambda i: (i, 0))
    return pl.pallas_call(
        _ln_rows_kernel,
        grid=(m // tr,),
        in_specs=[blk, _row_spec(l, D_MODEL, 1), _row_spec(l, D_MODEL, 1)],
        out_specs=[blk, blk],
        out_shape=[jax.ShapeDtypeStruct((m, D_MODEL), F32),
                   jax.ShapeDtypeStruct((m, D_MODEL), BF16)],
        compiler_params=_params("parallel"),
        name="ln_rows",
    )(z, _row_param(ln_g), _row_param(ln_b))


def _ln_f32_kernel(z_ref, g_ref, b_ref, y_ref):
    y_ref[...] = _layernorm(z_ref[...], g_ref[...], b_ref[...])


def _ln_rows_split(z, ln_g, ln_b, l, m_prompt, tr):
    assert z.shape[0] == m_prompt + SAMPLE_BLOCK and m_prompt % tr == 0

    def part(rows, tile, first_block):
        return pl.pallas_call(
            _ln_f32_kernel,
            grid=(rows // tile,),
            in_specs=[pl.BlockSpec((tile, D_MODEL), lambda i: (first_block + i, 0)),
                      _row_spec(l, D_MODEL, 1), _row_spec(l, D_MODEL, 1)],
            out_specs=pl.BlockSpec((tile, D_MODEL), lambda i: (i, 0)),
            out_shape=jax.ShapeDtypeStruct((rows, D_MODEL), F32),
            compiler_params=_params("parallel"),
            name="ln_rows_out",
        )(z, _row_param(ln_g), _row_param(ln_b))

    return part(m_prompt, tr, 0), part(SAMPLE_BLOCK, SAMPLE_BLOCK, m_prompt // SAMPLE_BLOCK)


def _sample_mix_kernel(h_ref, ha_ref, hb_ref, wa_ref, wb_ref, bias_ref, lbg_ref, lbb_ref,
                       ldg_ref, ldb_ref, ws0_ref, bs0_ref, cos_ref, sin_ref,
                       ya_in, yb_in, yd_in,
                       ya_ref, yb_ref, yd_ref, na_ref, nb_ref, vn_ref, qk_ref):
    del ya_in, yb_in, yd_in
    col = lambda c, w: h_ref[:, c:c + w]
    u_a = col(COL_A_C, W_A) * col(COL_A_H, W_A)
    conv_a = wa_ref[0:1, :] * ha_ref[0] + wa_ref[1:2, :] * ha_ref[1] + wa_ref[2:3, :] * u_a
    ya_ref[...] = (col(COL_A_B, W_A) * conv_a * _silu(col(COL_A_Z, W_A))).astype(BF16)
    na_ref[0] = ha_ref[1]
    na_ref[1] = u_a
    u_b = col(COL_B_A, W_B) * jax.nn.sigmoid(col(COL_B_G, W_B))
    cb = wb_ref[CONV_B - 1:CONV_B, :] * u_b + bias_ref[...]
    for k in range(CONV_B - 1):
        cb = cb + wb_ref[k:k + 1, :] * hb_ref[k]
    ln_b = _layernorm(cb, lbg_ref[...], lbb_ref[...])
    yb_ref[...] = (_silu(ln_b) * _silu(col(COL_B_Z, W_B))).astype(BF16)
    for k in range(CONV_B - 2):
        nb_ref[k] = hb_ref[k + 1]
    nb_ref[CONV_B - 2] = u_b
    vn = _layernorm(col(COL_D_V, W_D), ldg_ref[...], ldb_ref[...])
    vn_ref[...] = vn
    mix = ws0_ref[...] * vn + bs0_ref[...]
    yd_ref[...] = (col(COL_D_U, W_D) * mix * _silu(col(COL_D_Z, W_D))).astype(BF16)
    cos = cos_ref[...]
    sin = sin_ref[...]
    for hh in range(2 * N_HEADS):
        cs = slice(hh * HEAD_DIM, (hh + 1) * HEAD_DIM)
        qk_ref[:, cs] = _rope(h_ref[:, COL_Q + hh * HEAD_DIM:COL_Q + (hh + 1) * HEAD_DIM], cos, sin)


def _sample_mix(h, m_prompt, ys, hist_a, hist_b, conv_a_w, conv_b_w, conv_b_bias, ln_b_g, ln_b_b,
                ln_d_g, ln_d_b, w_s, b_s, cos, sin, l):
    r = SAMPLE_BLOCK
    blk = m_prompt // r
    ws0 = jnp.repeat(w_s[l, :, 0, 0], LANES)[None, :]
    bs0 = jnp.repeat(b_s[l, :, 0], LANES)[None, :]
    row = lambda p: p[l][None, :]
    small = [hist_a, hist_b, conv_a_w[l], conv_b_w[l], row(conv_b_bias), row(ln_b_g), row(ln_b_b),
             row(ln_d_g), row(ln_d_b), ws0, bs0, cos, sin]
    whole = lambda a: pl.BlockSpec(a.shape, lambda i, nd=a.ndim: (0,) * nd)
    y_spec = lambda w: pl.BlockSpec((r, w), lambda i: (blk, 0))
    f = lambda shape: jax.ShapeDtypeStruct(shape, F32)
    n_in = 1 + len(small)
    return pl.pallas_call(
        _sample_mix_kernel,
        grid=(1,),
        in_specs=([pl.BlockSpec((r, N_BRANCH_COLS), lambda i: (blk, 0))]
                  + [whole(a) for a in small]
                  + [pl.BlockSpec(memory_space=pl.ANY)] * 3),
        out_specs=[y_spec(W_A), y_spec(W_B), y_spec(W_D),
                   whole(f((CONV_A - 1, r, W_A))), whole(f((CONV_B - 1, r, W_B))),
                   whole(f((r, W_D))), whole(f((r, 2 * W_QKV)))],
        out_shape=[jax.ShapeDtypeStruct(ys[0].shape, BF16), jax.ShapeDtypeStruct(ys[1].shape, BF16),
                   jax.ShapeDtypeStruct(ys[2].shape, BF16),
                   f((CONV_A - 1, r, W_A)), f((CONV_B - 1, r, W_B)), f((r, W_D)), f((r, 2 * W_QKV))],
        input_output_aliases={n_in: 0, n_in + 1: 1, n_in + 2: 2},
        compiler_params=_params("arbitrary"),
        name="sample_mix",
    )(h, *small, *ys)


def _sample_attn_kernel(qk_ref, v_ref, cz_ref, c0_ref, c1_ref, c2_ref, yc_in, y_ref,
                        acc_ref, yrow_ref):
    del yc_in
    b = pl.program_id(0)
    scale = HEAD_DIM ** -0.5

    @pl.when(b == 0)
    def _():
        acc_ref[...] = jnp.zeros(acc_ref.shape, F32)

    os_, ls_ = [], []
    for g, c_ref in enumerate((c0_ref, c1_ref, c2_ref)):
        q = qk_ref[b, pl.ds(g * HEADS_PER_GROUP, HEADS_PER_GROUP), :]
        k_new = qk_ref[b, pl.ds(N_HEADS + g * HEADS_PER_GROUP, HEADS_PER_GROUP), :]
        v_new = v_ref[b, pl.ds(g * HEADS_PER_GROUP, HEADS_PER_GROUP), :]
        kc = c_ref[:, 0]
        vc = c_ref[:, 1]
        s_c = jnp.sum(kc * q[None], axis=-1, keepdims=True) * scale
        s_n = jnp.sum(k_new * q, axis=-1, keepdims=True) * scale
        mx = jnp.maximum(jnp.max(s_c, axis=0), s_n)
        p_c = jnp.exp(s_c - mx[None])
        p_n = jnp.exp(s_n - mx)
        den = jnp.sum(p_c, axis=0) + p_n
        os_.append((jnp.sum(p_c * vc, axis=0) + p_n * v_new) / den)
        ls_.append(mx + jnp.log(den))
    yrow_ref[...] = _merge_groups(os_, ls_) * _silu(cz_ref[b])
    is_b = lax.broadcasted_iota(jnp.int32, (acc_ref.shape[0], HEAD_DIM), 0) == b
    for hs in range(HEADS_PER_GROUP):
        cs = slice(hs * HEAD_DIM, (hs + 1) * HEAD_DIM)
        acc_ref[:, cs] = jnp.where(is_b, jnp.broadcast_to(yrow_ref[hs:hs + 1, :], is_b.shape),
                                   acc_ref[:, cs])

    @pl.when(b == pl.num_programs(0) - 1)
    def _():
        y_ref[...] = acc_ref[...].astype(BF16)


def _sample_attn(qk, h_s, caches, y_c, m_prompt, l, nb):
    r = SAMPLE_BLOCK
    qk3 = qk.reshape(r, 2 * N_HEADS, HEAD_DIM)
    v3 = h_s[:, COL_V:COL_V + W_QKV].reshape(r, N_HEADS, HEAD_DIM)
    cz3 = h_s[:, COL_C_Z:COL_C_Z + W_C].reshape(r, HEADS_PER_GROUP, HEAD_DIM)
    whole = lambda a: pl.BlockSpec(a.shape, lambda b, nd=a.ndim: (0,) * nd)
    cspec = pl.BlockSpec((None, None, SPAN, None, 2, HEADS_PER_GROUP, HEAD_DIM),
                         lambda b: (l, b, 0, 0, 0, 0, 0))
    return pl.pallas_call(
        _sample_attn_kernel,
        grid=(nb,),
        in_specs=[whole(qk3), whole(v3), whole(cz3), cspec, cspec, cspec,
                  pl.BlockSpec(memory_space=pl.ANY)],
        out_specs=pl.BlockSpec((r, W_C), lambda b: (m_prompt // r, 0)),
        out_shape=jax.ShapeDtypeStruct(y_c.shape, BF16),
        scratch_shapes=[pltpu.VMEM((r, W_C), F32), pltpu.VMEM((HEADS_PER_GROUP, HEAD_DIM), F32)],
        input_output_aliases={6: 0},
        compiler_params=_params("arbitrary"),
        name="sample_attn",
    )(qk3, v3, cz3, *caches, y_c)


def _kv_rows(k_rot, h, nb, s, g):
    keep = k_rot.shape[1]
    c0 = COL_V + g * W_C
    v = h[:nb * s, c0:c0 + W_C].reshape(nb, s, W_C)[:, -keep:]
    return jnp.stack([k_rot, v], axis=2).reshape(nb, keep, 2, HEADS_PER_GROUP, HEAD_DIM)


def _cache_view(c, dil):
    assert c.shape[2] == SPAN * dil
    return c.reshape(c.shape[0], c.shape[1], SPAN, dil, 2, HEADS_PER_GROUP, HEAD_DIM)


def kernel(x_prompt, x_sample, state_conv_a, state_conv_b, cache_kv_w128, cache_kv_w512,
           cache_kv_w2048, w_in, conv_a_w, conv_b_w, conv_b_bias, ln_b_g, ln_b_b, ln_d_g,
           ln_d_b, w_s, b_s, w_out_a, w_out_b, w_out_c, w_out_d, b_gate, w_o, ln_g, ln_b):
    bp, tp, _ = x_prompt.shape
    bs, ts, _ = x_sample.shape
    assert ts == 1 and bs <= SAMPLE_BLOCK
    mp = bp * tp
    assert mp % SAMPLE_BLOCK == 0
    m_all = mp + SAMPLE_BLOCK
    pad = SAMPLE_BLOCK - bs
    tiles = _tiles(m_all)

    w_o_b = w_o.astype(BF16)
    wouts = (w_out_a, w_out_b, w_out_c, w_out_d)
    caches = [_cache_view(c, d)
              for c, d in zip((cache_kv_w128, cache_kv_w512, cache_kv_w2048), DILATIONS)]

    cos_p, sin_p = _rope_tables(jnp.arange(tp, dtype=jnp.int32))
    cos_s, sin_s = _rope_tables(PAST_LEN + jnp.arange(ts, dtype=jnp.int32))

    x = jnp.concatenate([x_prompt.reshape(mp, D_MODEL), x_sample.reshape(bs, D_MODEL),
                         jnp.zeros((pad, D_MODEL), F32)], axis=0)
    xb = x.astype(BF16)

    ca_p, ca_s, cb_p, cb_s, dv_s = [], [], [], [], []
    kv_p = [[] for _ in range(N_GROUPS)]
    kv_s = [[] for _ in range(N_GROUPS)]
    for l in range(DEPTH):
        h = _in_proj(xb, w_in, l, tiles["in_proj"], 512)
        y_a, st_a = _branch_a(h, bp, tp, conv_a_w, l, 512)
        y_b, st_b, w_gate_b = _branch_b(h, bp, tp, conv_b_w, conv_b_bias, ln_b_g, ln_b_b,
                                        w_in, l, 256)
        y_d, *wouts_b = _branch_d(h, bp, tp, ln_d_g, ln_d_b, w_s, b_s, wouts, l, 512)
        o0, l0, kr0 = _attn_group(h, bp, tp, cos_p, sin_p, 0, 8)
        o1, l1, kr1 = _attn_group(h, bp, tp, cos_p, sin_p, 1, 2)
        y_c, kr2 = _attn_group(h, bp, tp, cos_p, sin_p, 2, 1, others=(o0, l0, o1, l1))
        ca_p.append(st_a)
        cb_p.append(st_b)
        for g, kr in enumerate((kr0, kr1, kr2)):
            kv_p[g].append(_kv_rows(kr, h, bp, tp, g))

        hist_a = jnp.pad(state_conv_a[l].transpose(1, 0, 2), ((0, 0), (0, pad), (0, 0)))
        hist_b = jnp.pad(state_conv_b[l].transpose(1, 0, 2), ((0, 0), (0, pad), (0, 0)))
        y_a, y_b, y_d, na, nb_, vn, qk_s = _sample_mix(
            h, mp, (y_a, y_b, y_d), hist_a, hist_b, conv_a_w, conv_b_w, conv_b_bias,
            ln_b_g, ln_b_b, ln_d_g, ln_d_b, w_s, b_s, cos_s, sin_s, l)
        h_s = h[mp:]
        y_c = _sample_attn(qk_s, h_s, caches, y_c, mp, l, bs)
        ca_s.append(na[:, :bs].transpose(1, 0, 2))
        cb_s.append(nb_[:, :bs].transpose(1, 0, 2))
        dv_s.append(vn[:bs].reshape(bs, ts, W_D))
        for g in range(N_GROUPS):
            c0 = g * W_C
            k = qk_s[:bs, W_QKV + c0:W_QKV + c0 + W_C]
            v = h_s[:bs, COL_V + c0:COL_V + c0 + W_C]
            kv_s[g].append(jnp.stack([k, v], axis=1).reshape(bs, ts, 2, HEADS_PER_GROUP, HEAD_DIM))

        merged = _gate_out(xb, w_gate_b, b_gate, (y_a, y_b, y_c, y_d), wouts_b, l,
                           tiles["gate_out"], 256)
        z = _proj_res(merged, w_o_b, x, l, tiles["proj_res"], 512)
        if l + 1 < DEPTH:
            x, xb = _ln_rows(z, ln_g, ln_b, l, tiles["ln_rows"])
        else:
            y_p, y_s = _ln_rows_split(z, ln_g, ln_b, l, mp, tiles["ln_out"])

    return (y_p.reshape(bp, tp, D_MODEL), y_s[:bs].reshape(bs, ts, D_MODEL),
            jnp.stack(ca_p), jnp.stack(ca_s), jnp.stack(cb_p), jnp.stack(cb_s),
            jnp.stack(kv_p[0]), jnp.stack(kv_s[0]), jnp.stack(kv_p[1]), jnp.stack(kv_s[1]),
            jnp.stack(kv_p[2]), jnp.stack(kv_s[2]), jnp.stack(dv_s))
```

```python
import functools

import jax
import jax.numpy as jnp
from jax import lax
from jax.experimental import pallas as pl
from jax.experimental.pallas import tpu as pltpu

D_MODEL = 4096
DEPTH = 2
PAST_LEN = 16384
W_A = 1024
W_B = 1024
HEAD_DIM = 128
N_GROUPS = 3
HEADS_PER_GROUP = 4
N_HEADS = N_GROUPS * HEADS_PER_GROUP
W_QKV = N_HEADS * HEAD_DIM
W_C = HEADS_PER_GROUP * HEAD_DIM
W_D = 1024
CHUNK = 128
D_GROUPS = 8
CONV_A = 3
CONV_B = 31
WINDOWS = (128, 512, 2048)
DILATIONS = (1, 4, 16)
SPAN = 128
ROPE_THETA = 10000.0
N_BRANCH = 4
LN_EPS = 1e-5
ALPHA = (2 * DEPTH) ** 0.25

COL_A_B, COL_A_C, COL_A_H, COL_A_Z = 0, 1024, 2048, 3072
COL_B_A, COL_B_G, COL_B_Z = 4096, 5120, 6144
COL_Q, COL_K, COL_V, COL_C_Z = 7168, 8704, 10240, 11776
COL_D_U, COL_D_V, COL_D_Z = 12288, 13312, 14336
N_BRANCH_COLS = 15360
COL_GATE = 15360

LANES = 128
SUBLANES = 8
SAMPLE_BLOCK = 64
VMEM_LIMIT_BYTES = 56 * 1024 * 1024

F32 = jnp.float32
BF16 = jnp.bfloat16


def _params(*sem):
    return pltpu.CompilerParams(dimension_semantics=sem, vmem_limit_bytes=VMEM_LIMIT_BYTES)


def _silu(v):
    return v * jax.nn.sigmoid(v)


def _layernorm(v, g, b):
    mu = jnp.mean(v, axis=-1, keepdims=True)
    c = v - mu
    var = jnp.mean(c * c, axis=-1, keepdims=True)
    return c * lax.rsqrt(var + LN_EPS) * g + b


def _row_param(p):
    return p.reshape(DEPTH, 1, p.shape[-1])


def _row_spec(l, c, nargs):
    if nargs == 1:
        return pl.BlockSpec((None, 1, c), lambda b: (l, 0, 0))
    return pl.BlockSpec((None, 1, c), lambda b, t: (l, 0, 0))


def _tiles(m_all):
    def largest(limit, mult):
        return max(t for t in range(mult, limit + 1, mult) if m_all % t == 0)
    m_prompt = m_all - SAMPLE_BLOCK
    return dict(in_proj=largest(2176, 16), gate_out=largest(768, 16),
                proj_res=largest(1536, 16), ln_rows=largest(384, 16),
                ln_out=max(t for t in (512, 256, 128, 64) if m_prompt % t == 0))


def _mm_kernel(x_ref, w_ref, o_ref):
    o_ref[...] = jnp.dot(x_ref[...], w_ref[...].astype(BF16), preferred_element_type=F32)


def _in_proj(xb, w_in, l, tm, tn):
    m = xb.shape[0]
    return pl.pallas_call(
        _mm_kernel,
        grid=(m // tm, N_BRANCH_COLS // tn),
        in_specs=[pl.BlockSpec((tm, D_MODEL), lambda i, j: (i, 0), pipeline_mode=pl.Buffered(1)),
                  pl.BlockSpec((None, D_MODEL, tn), lambda i, j: (l, 0, j))],
        out_specs=pl.BlockSpec((tm, tn), lambda i, j: (i, j)),
        out_shape=jax.ShapeDtypeStruct((m, N_BRANCH_COLS), F32),
        compiler_params=_params("parallel", "parallel"),
        name="in_proj",
    )(xb, w_in)


A_PAD = 8


def _branch_a_kernel(ab_ref, ac_ref, ah_ref, az_ref, w_ref, y_init, y_ref, st_ref, ext_ref):
    del y_init
    t = pl.program_id(1)
    tt = ab_ref.shape[0]

    @pl.when(t == 0)
    def _():
        ext_ref[0:A_PAD, :] = jnp.zeros((A_PAD, W_A), F32)

    @pl.when(t > 0)
    def _():
        ext_ref[0:A_PAD, :] = ext_ref[tt:tt + A_PAD, :]

    u = ac_ref[...] * ah_ref[...]
    ext_ref[A_PAD:A_PAD + tt, :] = u
    w = w_ref[...]
    conv = (w[0:1, :] * ext_ref[A_PAD - 2:A_PAD - 2 + tt, :]
            + w[1:2, :] * ext_ref[A_PAD - 1:A_PAD - 1 + tt, :]
            + w[2:3, :] * u)
    y_ref[...] = (ab_ref[...] * conv * _silu(az_ref[...])).astype(BF16)
    st_ref[...] = ext_ref[A_PAD + tt - (CONV_A - 1):A_PAD + tt, :]


ANY_SPEC = pl.BlockSpec(memory_space=pl.ANY)


def _branch_a(h, nb, s, conv_a_w, y_init, l, tt):
    nt = s // tt
    col = lambda c: pl.BlockSpec((tt, W_A), lambda b, t: (b * nt + t, c // W_A))
    return pl.pallas_call(
        _branch_a_kernel,
        grid=(nb, nt),
        in_specs=[col(COL_A_B), col(COL_A_C), col(COL_A_H), col(COL_A_Z),
                  pl.BlockSpec((None, CONV_A, W_A), lambda b, t: (l, 0, 0)), ANY_SPEC],
        input_output_aliases={5: 0},
        out_specs=[pl.BlockSpec((tt, W_A), lambda b, t: (b * nt + t, 0)),
                   pl.BlockSpec((None, CONV_A - 1, W_A), lambda b, t: (b, 0, 0))],
        out_shape=[jax.ShapeDtypeStruct((h.shape[0], W_A), BF16),
                   jax.ShapeDtypeStruct((nb, CONV_A - 1, W_A), F32)],
        scratch_shapes=[pltpu.VMEM((A_PAD + tt, W_A), F32)],
        compiler_params=_params("parallel", "arbitrary"),
        name="branch_a",
    )(h, h, h, h, conv_a_w, y_init)


B_PAD = 32
B_ROWS = 64


def _branch_b_kernel(ba_ref, bg_ref, bz_ref, w_ref, bias_ref, g_ref, b_ref, wg_ref, y_init,
                     y_ref, st_ref, wgb_ref, ext_ref, sh_ref, cb_ref):
    del y_init
    t = pl.program_id(1)
    tt = ba_ref.shape[0]
    wgb_ref[...] = wg_ref[...].astype(BF16)

    @pl.when(t == 0)
    def _():
        ext_ref[0:B_PAD, :] = jnp.zeros((B_PAD, W_B), F32)

    @pl.when(t > 0)
    def _():
        ext_ref[0:B_PAD, :] = ext_ref[tt:tt + B_PAD, :]

    ext_ref[B_PAD:B_PAD + tt, :] = ba_ref[...] * jax.nn.sigmoid(bg_ref[...])
    sh_rows = sh_ref.shape[1]
    for s in range(1, SUBLANES):
        sh_ref[s - 1] = ext_ref[s:s + sh_rows, :]
    first = B_PAD - (CONV_B - 1)
    for c in range(W_B // LANES):
        cs = slice(c * LANES, (c + 1) * LANES)
        wc = w_ref[:, cs]
        for rc in range(tt // B_ROWS):
            acc = jnp.zeros((B_ROWS, LANES), F32)
            for k in range(CONV_B):
                q, s = divmod(first + k, SUBLANES)
                r0 = SUBLANES * q + rc * B_ROWS
                if s == 0:
                    tap = ext_ref[r0:r0 + B_ROWS, cs]
                else:
                    tap = sh_ref[s - 1, r0:r0 + B_ROWS, cs]
                acc = acc + wc[k:k + 1, :] * tap
            cb_ref[rc * B_ROWS:(rc + 1) * B_ROWS, cs] = acc
    cb = cb_ref[...] + bias_ref[...]
    ln = _layernorm(cb, g_ref[...], b_ref[...])
    y_ref[...] = (_silu(ln) * _silu(bz_ref[...])).astype(BF16)
    st_ref[...] = ext_ref[B_PAD + tt - (CONV_B - 1):B_PAD + tt, :]


def _branch_b(h, nb, s, conv_b_w, conv_b_bias, ln_b_g, ln_b_b, w_in, y_init, l, tt):
    nt = s // tt
    n_gate = N_BRANCH * D_MODEL
    wcols = n_gate // (nb * nt)
    assert wcols % LANES == 0 and COL_GATE % wcols == 0
    col = lambda c: pl.BlockSpec((tt, W_B), lambda b, t: (b * nt + t, c // W_B))
    return pl.pallas_call(
        _branch_b_kernel,
        grid=(nb, nt),
        in_specs=[col(COL_B_A), col(COL_B_G), col(COL_B_Z),
                  pl.BlockSpec((None, CONV_B, W_B), lambda b, t: (l, 0, 0)),
                  _row_spec(l, W_B, 2), _row_spec(l, W_B, 2), _row_spec(l, W_B, 2),
                  pl.BlockSpec((None, D_MODEL, wcols),
                               lambda b, t: (l, 0, COL_GATE // wcols + b * nt + t)), ANY_SPEC],
        input_output_aliases={8: 0},
        out_specs=[pl.BlockSpec((tt, W_B), lambda b, t: (b * nt + t, 0)),
                   pl.BlockSpec((None, CONV_B - 1, W_B), lambda b, t: (b, 0, 0)),
                   pl.BlockSpec((D_MODEL, wcols), lambda b, t: (0, b * nt + t))],
        out_shape=[jax.ShapeDtypeStruct((h.shape[0], W_B), BF16),
                   jax.ShapeDtypeStruct((nb, CONV_B - 1, W_B), F32),
                   jax.ShapeDtypeStruct((D_MODEL, n_gate), BF16)],
        scratch_shapes=[pltpu.VMEM((B_PAD + tt, W_B), F32),
                        pltpu.VMEM((SUBLANES - 1, B_PAD + tt - SUBLANES, W_B), F32),
                        pltpu.VMEM((tt, W_B), F32)],
        compiler_params=_params("parallel", "arbitrary"),
        name="branch_b",
    )(h, h, h, conv_b_w, _row_param(conv_b_bias), _row_param(ln_b_g), _row_param(ln_b_b), w_in,
      y_init)


def _branch_d_kernel(du_ref, dv_ref, dz_ref, g_ref, b_ref, ws_ref, bs_ref,
                     wa_ref, wb_ref, wc_ref, wd_ref, y_init,
                     y_ref, wab_ref, wbb_ref, wcb_ref, wdb_ref):
    del y_init
    tt = du_ref.shape[0]
    for src, dst in ((wa_ref, wab_ref), (wb_ref, wbb_ref), (wc_ref, wcb_ref), (wd_ref, wdb_ref)):
        dst[...] = src[...].astype(BF16)
    vb = _layernorm(dv_ref[...], g_ref[...], b_ref[...]).astype(BF16)
    row = lax.broadcasted_iota(jnp.int32, (CHUNK, CHUNK), 0)
    col = lax.broadcasted_iota(jnp.int32, (CHUNK, CHUNK), 1)
    causal = col <= row
    for g in range(D_GROUPS):
        cs = slice(g * LANES, (g + 1) * LANES)
        wm = jnp.where(causal, ws_ref[g], 0.0).astype(BF16)
        for c in range(tt // CHUNK):
            rs = slice(c * CHUNK, (c + 1) * CHUNK)
            mix = jnp.dot(wm, vb[rs, cs], preferred_element_type=F32) + bs_ref[:, cs]
            y_ref[rs, cs] = (du_ref[rs, cs] * mix * _silu(dz_ref[rs, cs])).astype(BF16)


def _bias_slab(b_s_l):
    return jnp.repeat(b_s_l.T, LANES, axis=1)


def _branch_d(h, nb, s, ln_d_g, ln_d_b, w_s, b_s, wouts, y_init, l, tt):
    nt = s // tt
    steps = nb * nt
    col = lambda c: pl.BlockSpec((tt, W_D), lambda b, t: (b * nt + t, c // W_D))
    w_in_specs, w_out_specs, w_out_shapes = [], [], []
    for w in wouts:
        rows = w.shape[1] // steps
        assert rows % 16 == 0 and rows * steps == w.shape[1]
        w_in_specs.append(pl.BlockSpec((None, rows, D_MODEL), lambda b, t: (l, b * nt + t, 0)))
        w_out_specs.append(pl.BlockSpec((rows, D_MODEL), lambda b, t: (b * nt + t, 0)))
        w_out_shapes.append(jax.ShapeDtypeStruct((w.shape[1], D_MODEL), BF16))
    return pl.pallas_call(
        _branch_d_kernel,
        grid=(nb, nt),
        in_specs=[col(COL_D_U), col(COL_D_V), col(COL_D_Z),
                  _row_spec(l, W_D, 2), _row_spec(l, W_D, 2),
                  pl.BlockSpec((None, D_GROUPS, CHUNK, CHUNK), lambda b, t: (l, 0, 0, 0)),
                  pl.BlockSpec((CHUNK, W_D), lambda b, t: (0, 0))] + w_in_specs + [ANY_SPEC],
        input_output_aliases={7 + len(wouts): 0},
        out_specs=[pl.BlockSpec((tt, W_D), lambda b, t: (b * nt + t, 0))] + w_out_specs,
        out_shape=[jax.ShapeDtypeStruct((h.shape[0], W_D), BF16)] + w_out_shapes,
        compiler_params=_params("parallel", "parallel"),
        name="branch_d",
    )(h, h, h, _row_param(ln_d_g), _row_param(ln_d_b), w_s, _bias_slab(b_s[l]), *wouts, y_init)


def _rope_tables(pos):
    half = HEAD_DIM // 2
    inv = ROPE_THETA ** (-jnp.arange(half, dtype=F32) / half)
    ang = pos.astype(F32)[:, None] * inv[None, :]
    cos = jnp.cos(ang)
    sin = jnp.sin(ang)
    return jnp.concatenate([cos, cos], axis=-1), jnp.concatenate([-sin, sin], axis=-1)


def _rope(v, cos, sin):
    return v * cos + pltpu.roll(v, HEAD_DIM // 2, v.ndim - 1) * sin


def _merge_groups(os_, ls_):
    mx = jnp.maximum(jnp.maximum(ls_[0], ls_[1]), ls_[2])
    es = [jnp.exp(l_ - mx) for l_ in ls_]
    den = es[0] + es[1] + es[2]
    return (es[0] / den) * os_[0] + (es[1] / den) * os_[1] + (es[2] / den) * os_[2]


ATTN_ILP = 8


PERM_MIN_DIL = SUBLANES


def _attn_kernel(*refs, dil, blocks, use_prev, keep, merge, perm):
    refs = list(refs)
    q_ref, kc_ref, vc_ref, cos_ref, sin_ref = refs[:5]
    del refs[:5]
    if use_prev:
        kp_ref, vp_ref, cosp_ref, sinp_ref = refs[:4]
        del refs[:4]
    if perm:
        p_ref = refs.pop(0)
    if merge:
        oa_ref, la_ref, ob_ref, lb_ref, cz_ref, _y_init, y_ref = refs[:7]
        del refs[:7]
    else:
        o_ref, l_ref = refs[:2]
        del refs[:2]
    kr_ref, qs_ref, ks_ref = refs[:3]
    del refs[:3]
    if use_prev:
        kps_ref = refs.pop(0)
    if perm:
        qd_ref, kd_ref, vd_ref = refs[:3]
        del refs[:3]
        if use_prev:
            kpd_ref, vpd_ref = refs[:2]
            del refs[:2]
    if merge:
        o_ref, l_ref = refs[:2]
        del refs[:2]

    n = pl.program_id(2)
    tile = q_ref.shape[0]
    scale = HEAD_DIM ** -0.5
    cos = cos_ref[...]
    sin = sin_ref[...]
    qs_ref[...] = _rope(q_ref[...], cos, sin)
    ks_ref[...] = _rope(kc_ref[...], cos, sin)
    kr_ref[...] = ks_ref[tile - keep:tile, :]
    if use_prev:
        kps_ref[...] = _rope(kp_ref[...], cosp_ref[...], sinp_ref[...])
    if perm:
        pmat = p_ref[...]

        def deinterleave(src_ref, dst_ref):
            for s0 in range(0, src_ref.shape[0], SPAN):
                dst_ref[s0:s0 + SPAN, :] = jnp.dot(pmat, src_ref[s0:s0 + SPAN, :].astype(BF16),
                                                   preferred_element_type=F32)

        deinterleave(qs_ref, qd_ref)
        deinterleave(ks_ref, kd_ref)
        deinterleave(vc_ref, vd_ref)
        if use_prev:
            deinterleave(kps_ref, kpd_ref)
            deinterleave(vp_ref, vpd_ref)

    width = 2 * SPAN if use_prev else SPAN
    row = lax.broadcasted_iota(jnp.int32, (SPAN, width), 0)
    col = lax.broadcasted_iota(jnp.int32, (SPAN, width), 1)
    if use_prev:
        band = jnp.logical_or(jnp.logical_and(col < SPAN, col >= row),
                              jnp.logical_and(col >= SPAN, col - SPAN <= row))
        has_prev = jnp.zeros((SPAN, width), jnp.int32) + n > 0
        first = jnp.logical_and(band, jnp.logical_or(col >= SPAN, has_prev))
    else:
        band = col <= row
        first = band
    nt = (((1,), (1,)), ((), ()))

    def rows(j, r, count=SPAN):
        start = j * SPAN * dil + r
        return pl.ds(start, count) if dil == 1 else pl.ds(start, count, stride=dil)

    piece = SPAN // dil

    def take(ref, slab0, n_slabs, r):
        if not perm:
            return ref[rows(slab0 // dil, r, n_slabs // dil * SPAN), :]
        return jnp.concatenate(
            [ref[(slab0 + c) * SPAN + r * piece:(slab0 + c) * SPAN + (r + 1) * piece, :]
             for c in range(n_slabs)], axis=0)

    q_src, k_src, v_src = (qd_ref, kd_ref, vd_ref) if perm else (qs_ref, ks_ref, vc_ref)
    if use_prev:
        kp_src, vp_src = (kpd_ref, vpd_ref) if perm else (kps_ref, vp_ref)

    def load_kv(j, r):
        if not use_prev:
            return (take(k_src, j * dil, dil, r).astype(BF16),
                    take(v_src, j * dil, dil, r).astype(BF16))
        if j > 0:
            return (take(k_src, (j - 1) * dil, 2 * dil, r).astype(BF16),
                    take(v_src, (j - 1) * dil, 2 * dil, r).astype(BF16))
        k2 = jnp.concatenate([take(kp_src, 0, dil, r), take(k_src, 0, dil, r)], axis=0)
        v2 = jnp.concatenate([take(vp_src, 0, dil, r), take(v_src, 0, dil, r)], axis=0)
        return k2.astype(BF16), v2.astype(BF16)

    def run(grp):
        qs = [take(q_src, j * dil, dil, r).astype(BF16) for r, j in grp]
        kvs = [load_kv(j, r) for r, j in grp]
        ss = [lax.dot_general(q, k2, nt, preferred_element_type=F32) * scale
              for q, (k2, _) in zip(qs, kvs)]
        ss = [jnp.where(first if j == 0 else band, s, -jnp.inf) for s, (r, j) in zip(ss, grp)]
        mxs = [jnp.max(s, axis=1, keepdims=True) for s in ss]
        ps = [jnp.exp(s - mx) for s, mx in zip(ss, mxs)]
        dens = [jnp.sum(p, axis=1, keepdims=True) for p in ps]
        outs = [jnp.dot((p / den).astype(BF16), v2, preferred_element_type=F32)
                for p, den, (_, v2) in zip(ps, dens, kvs)]
        for (r, j), o, mx, den in zip(grp, outs, mxs, dens):
            idx = rows(j, r)
            o_ref[idx, :] = o
            l_ref[idx, :] = jnp.broadcast_to(mx + jnp.log(den), (SPAN, HEAD_DIM))

    order = [(r, j) for r in range(dil) for j in range(blocks)]
    for i in range(0, len(order), ATTN_ILP):
        run(order[i:i + ATTN_ILP])
    if merge:
        om = _merge_groups([oa_ref[...], ob_ref[...], o_ref[...]],
                           [la_ref[...], lb_ref[...], l_ref[...]])
        y_ref[...] = (om * _silu(cz_ref[...])).astype(BF16)


def _slab_permutation(dil):
    piece = SPAN // dil
    out_row = jnp.arange(SPAN)
    src = (out_row % piece) * dil + out_row // piece
    return (src[:, None] == jnp.arange(SPAN)[None, :]).astype(BF16)


def _attn_group(h, nb, s, cos, sin, g, blocks, others=None, y_init=None):
    dil = DILATIONS[g]
    unit = SPAN * dil
    tile = unit * blocks
    use_prev = s > unit
    per = tile // unit
    nt, nu = s // tile, s // unit
    keep = min(WINDOWS[g], s)
    assert keep <= tile and s % tile == 0
    merge = others is not None
    aliases = {}
    head = lambda c0: (lambda hs: c0 // HEAD_DIM + g * HEADS_PER_GROUP + hs)
    cq, ck, cv = head(COL_Q), head(COL_K), head(COL_V)
    prev = lambda n: jnp.maximum(n * per - 1, 0)
    col_blk = lambda rows_, fn: pl.BlockSpec((rows_, HEAD_DIM), fn)
    own = lambda c: col_blk(tile, lambda b, hs, n: (b * nt + n, c(hs)))
    in_specs = [own(cq), own(ck), own(cv),
                col_blk(tile, lambda b, hs, n: (n, 0)), col_blk(tile, lambda b, hs, n: (n, 0))]
    args = [h, h, h, cos, sin]
    if use_prev:
        before = lambda c: col_blk(unit, lambda b, hs, n: (b * nu + prev(n), c(hs)))
        in_specs += [before(ck), before(cv),
                     col_blk(unit, lambda b, hs, n: (prev(n), 0)),
                     col_blk(unit, lambda b, hs, n: (prev(n), 0))]
        args += [h, h, cos, sin]
    perm = dil >= PERM_MIN_DIL
    if perm:
        in_specs.append(pl.BlockSpec((SPAN, SPAN), lambda b, hs, n: (0, 0)))
        args.append(_slab_permutation(dil))
    plain = col_blk(tile, lambda b, hs, n: (b * nt + n, hs))
    if merge:
        in_specs += [plain] * 4 + [own(lambda hs: COL_C_Z // HEAD_DIM + hs), ANY_SPEC]
        args += list(others) + [h, y_init]
        aliases = {len(args) - 1: 0}
        out_specs = [plain]
        out_shape = [jax.ShapeDtypeStruct((h.shape[0], W_C), BF16)]
    else:
        out_specs = [plain, plain]
        out_shape = [jax.ShapeDtypeStruct((nb * s, W_C), F32)] * 2
    out_specs.append(pl.BlockSpec((None, keep, HEAD_DIM), lambda b, hs, n: (b, 0, hs)))
    out_shape.append(jax.ShapeDtypeStruct((nb, keep, W_C), F32))
    buf = lambda rows_: pltpu.VMEM((rows_, HEAD_DIM), F32)
    scratch = [buf(tile), buf(tile)]
    if use_prev:
        scratch.append(buf(unit))
    if perm:
        scratch += [buf(tile)] * 3
        if use_prev:
            scratch += [buf(unit)] * 2
    if merge:
        scratch += [buf(tile)] * 2
    kern = functools.partial(_attn_kernel, dil=dil, blocks=blocks, use_prev=use_prev,
                             keep=keep, merge=merge, perm=perm)
    return pl.pallas_call(
        kern,
        grid=(nb, HEADS_PER_GROUP, nt),
        in_specs=in_specs,
        out_specs=out_specs,
        out_shape=out_shape,
        scratch_shapes=scratch,
        input_output_aliases=aliases,
        compiler_params=_params("parallel", "parallel", "arbitrary"),
        name=f"attn_g{g}",
    )(*args)


def _gate_out_kernel(x_ref, wg0, wg1, wg2, wg3, bg0, bg1, bg2, bg3,
                     ya, yb, yc, yd, woa, wob, woc, wod, o_ref):
    x = x_ref[...]
    acc = None
    for wg, bg, y, wo in ((wg0, bg0, ya, woa), (wg1, bg1, yb, wob),
                          (wg2, bg2, yc, woc), (wg3, bg3, yd, wod)):
        gate = jax.nn.sigmoid(jnp.dot(x, wg[...], preferred_element_type=F32) + bg[...])
        part = jnp.dot(y[...], wo[...], preferred_element_type=F32)
        acc = gate * part if acc is None else acc + gate * part
    o_ref[...] = acc.astype(BF16)


def _gate_out(xb, w_gate_b, b_gate, ys, wouts_b, l, tm, tn):
    m = xb.shape[0]
    nj = D_MODEL // tn
    wg_spec = lambda br: pl.BlockSpec((D_MODEL, tn), lambda i, j: (0, br * nj + j))
    bg_spec = lambda br: pl.BlockSpec((None, 1, tn), lambda i, j: (l, 0, br * nj + j))
    y_spec = lambda y: pl.BlockSpec((tm, y.shape[1]), lambda i, j: (i, 0))
    wo_spec = lambda w: pl.BlockSpec((w.shape[0], tn), lambda i, j: (0, j))
    bg3 = b_gate.reshape(DEPTH, 1, N_BRANCH * D_MODEL)
    return pl.pallas_call(
        _gate_out_kernel,
        grid=(m // tm, nj),
        in_specs=([pl.BlockSpec((tm, D_MODEL), lambda i, j: (i, 0))]
                  + [wg_spec(br) for br in range(N_BRANCH)]
                  + [bg_spec(br) for br in range(N_BRANCH)]
                  + [y_spec(y) for y in ys]
                  + [wo_spec(w) for w in wouts_b]),
        out_specs=pl.BlockSpec((tm, tn), lambda i, j: (i, j)),
        out_shape=jax.ShapeDtypeStruct((m, D_MODEL), BF16),
        compiler_params=_params("parallel", "parallel"),
        name="gate_out",
    )(xb, *([w_gate_b] * N_BRANCH), *([bg3] * N_BRANCH), *ys, *wouts_b)


def _proj_res_kernel(m_ref, wo_ref, x_ref, z_ref):
    z_ref[...] = ALPHA * x_ref[...] + jnp.dot(m_ref[...], wo_ref[...], preferred_element_type=F32)


def _proj_res(merged, w_o_b, x, l, tm, tn):
    m = x.shape[0]
    return pl.pallas_call(
        _proj_res_kernel,
        grid=(m // tm, D_MODEL // tn),
        in_specs=[pl.BlockSpec((tm, D_MODEL), lambda i, j: (i, 0)),
                  pl.BlockSpec((None, D_MODEL, tn), lambda i, j: (l, 0, j)),
                  pl.BlockSpec((tm, tn), lambda i, j: (i, j))],
        out_specs=pl.BlockSpec((tm, tn), lambda i, j: (i, j)),
        out_shape=jax.ShapeDtypeStruct((m, D_MODEL), F32),
        compiler_params=_params("parallel", "parallel"),
        name="proj_res",
    )(merged, w_o_b, x)


def _ln_rows_kernel(z_ref, g_ref, b_ref, y_ref, yb_ref):
    y = _layernorm(z_ref[...], g_ref[...], b_ref[...])
    y_ref[...] = y
    yb_ref[...] = y.astype(BF16)


def _ln_rows(z, ln_g, ln_b, l, tr):
    m = z.shape[0]
    blk = pl.BlockSpec((tr, D_MODEL), lambda i: (i, 0))
    return pl.pallas_call(
        _ln_rows_kernel,
        grid=(m // tr,),
        in_specs=[blk, _row_spec(l, D_MODEL, 1), _row_spec(l, D_MODEL, 1)],
        out_specs=[blk, blk],
        out_shape=[jax.ShapeDtypeStruct((m, D_MODEL), F32),
                   jax.ShapeDtypeStruct((m, D_MODEL), BF16)],
        compiler_params=_params("parallel"),
        name="ln_rows",
    )(z, _row_param(ln_g), _row_param(ln_b))


def _ln_f32_kernel(z_ref, g_ref, b_ref, y_ref):
    y_ref[...] = _layernorm(z_ref[...], g_ref[...], b_ref[...])


def _ln_rows_split(z, ln_g, ln_b, l, m_prompt, tr):
    assert z.shape[0] == m_prompt + SAMPLE_BLOCK and m_prompt % tr == 0

    def part(rows, tile, first_block):
        return pl.pallas_call(
            _ln_f32_kernel,
            grid=(rows // tile,),
            in_specs=[pl.BlockSpec((tile, D_MODEL), lambda i: (first_block + i, 0)),
                      _row_spec(l, D_MODEL, 1), _row_spec(l, D_MODEL, 1)],
            out_specs=pl.BlockSpec((tile, D_MODEL), lambda i: (i, 0)),
            out_shape=jax.ShapeDtypeStruct((rows, D_MODEL), F32),
            compiler_params=_params("parallel"),
            name="ln_rows_out",
        )(z, _row_param(ln_g), _row_param(ln_b))

    return part(m_prompt, tr, 0), part(SAMPLE_BLOCK, SAMPLE_BLOCK, m_prompt // SAMPLE_BLOCK)


def _sample_mix_kernel(h_ref, ha_ref, hb_ref, wa_ref, wb_ref, bias_ref, lbg_ref, lbb_ref,
                       ldg_ref, ldb_ref, ws0_ref, bs0_ref, cos_ref, sin_ref,
                       ya_in, yb_in, yd_in,
                       ya_ref, yb_ref, yd_ref, na_ref, nb_ref, vn_ref, qk_ref):
    del ya_in, yb_in, yd_in
    col = lambda c, w: h_ref[:, c:c + w]
    u_a = col(COL_A_C, W_A) * col(COL_A_H, W_A)
    conv_a = wa_ref[0:1, :] * ha_ref[0] + wa_ref[1:2, :] * ha_ref[1] + wa_ref[2:3, :] * u_a
    ya_ref[...] = (col(COL_A_B, W_A) * conv_a * _silu(col(COL_A_Z, W_A))).astype(BF16)
    na_ref[0] = ha_ref[1]
    na_ref[1] = u_a
    u_b = col(COL_B_A, W_B) * jax.nn.sigmoid(col(COL_B_G, W_B))
    cb = wb_ref[CONV_B - 1:CONV_B, :] * u_b + bias_ref[...]
    for k in range(CONV_B - 1):
        cb = cb + wb_ref[k:k + 1, :] * hb_ref[k]
    ln_b = _layernorm(cb, lbg_ref[...], lbb_ref[...])
    yb_ref[...] = (_silu(ln_b) * _silu(col(COL_B_Z, W_B))).astype(BF16)
    for k in range(CONV_B - 2):
        nb_ref[k] = hb_ref[k + 1]
    nb_ref[CONV_B - 2] = u_b
    vn = _layernorm(col(COL_D_V, W_D), ldg_ref[...], ldb_ref[...])
    vn_ref[...] = vn
    mix = ws0_ref[...] * vn + bs0_ref[...]
    yd_ref[...] = (col(COL_D_U, W_D) * mix * _silu(col(COL_D_Z, W_D))).astype(BF16)
    cos = cos_ref[...]
    sin = sin_ref[...]
    for hh in range(2 * N_HEADS):
        cs = slice(hh * HEAD_DIM, (hh + 1) * HEAD_DIM)
        qk_ref[:, cs] = _rope(h_ref[:, COL_Q + hh * HEAD_DIM:COL_Q + (hh + 1) * HEAD_DIM], cos, sin)


def _sample_mix(h, m_prompt, ys, hist_a, hist_b, conv_a_w, conv_b_w, conv_b_bias, ln_b_g, ln_b_b,
                ln_d_g, ln_d_b, w_s, b_s, cos, sin, l):
    r = SAMPLE_BLOCK
    blk = m_prompt // r
    ws0 = jnp.repeat(w_s[l, :, 0, 0], LANES)[None, :]
    bs0 = jnp.repeat(b_s[l, :, 0], LANES)[None, :]
    row = lambda p: p[l][None, :]
    small = [hist_a, hist_b, conv_a_w[l], conv_b_w[l], row(conv_b_bias), row(ln_b_g), row(ln_b_b),
             row(ln_d_g), row(ln_d_b), ws0, bs0, cos, sin]
    whole = lambda a: pl.BlockSpec(a.shape, lambda i, nd=a.ndim: (0,) * nd)
    y_spec = lambda w: pl.BlockSpec((r, w), lambda i: (blk, 0))
    f = lambda shape: jax.ShapeDtypeStruct(shape, F32)
    n_in = 1 + len(small)
    return pl.pallas_call(
        _sample_mix_kernel,
        grid=(1,),
        in_specs=([pl.BlockSpec((r, N_BRANCH_COLS), lambda i: (blk, 0))]
                  + [whole(a) for a in small]
                  + [pl.BlockSpec(memory_space=pl.ANY)] * 3),
        out_specs=[y_spec(W_A), y_spec(W_B), y_spec(W_D),
                   whole(f((CONV_A - 1, r, W_A))), whole(f((CONV_B - 1, r, W_B))),
                   whole(f((r, W_D))), whole(f((r, 2 * W_QKV)))],
        out_shape=[jax.ShapeDtypeStruct(ys[0].shape, BF16), jax.ShapeDtypeStruct(ys[1].shape, BF16),
                   jax.ShapeDtypeStruct(ys[2].shape, BF16),
                   f((CONV_A - 1, r, W_A)), f((CONV_B - 1, r, W_B)), f((r, W_D)), f((r, 2 * W_QKV))],
        input_output_aliases={n_in: 0, n_in + 1: 1, n_in + 2: 2},
        compiler_params=_params("arbitrary"),
        name="sample_mix",
    )(h, *small, *ys)


def _sample_attn_kernel(qk_ref, v_ref, cz_ref, c0_ref, c1_ref, c2_ref, yc_in, y_ref,
                        acc_ref, yrow_ref):
    del yc_in
    b = pl.program_id(0)
    scale = HEAD_DIM ** -0.5

    @pl.when(b == 0)
    def _():
        acc_ref[...] = jnp.zeros(acc_ref.shape, F32)

    os_, ls_ = [], []
    for g, c_ref in enumerate((c0_ref, c1_ref, c2_ref)):
        q = qk_ref[b, pl.ds(g * HEADS_PER_GROUP, HEADS_PER_GROUP), :]
        k_new = qk_ref[b, pl.ds(N_HEADS + g * HEADS_PER_GROUP, HEADS_PER_GROUP), :]
        v_new = v_ref[b, pl.ds(g * HEADS_PER_GROUP, HEADS_PER_GROUP), :]
        kc = c_ref[:, 0]
        vc = c_ref[:, 1]
        s_c = jnp.sum(kc * q[None], axis=-1, keepdims=True) * scale
        s_n = jnp.sum(k_new * q, axis=-1, keepdims=True) * scale
        mx = jnp.maximum(jnp.max(s_c, axis=0), s_n)
        p_c = jnp.exp(s_c - mx[None])
        p_n = jnp.exp(s_n - mx)
        den = jnp.sum(p_c, axis=0) + p_n
        os_.append((jnp.sum(p_c * vc, axis=0) + p_n * v_new) / den)
        ls_.append(mx + jnp.log(den))
    yrow_ref[...] = _merge_groups(os_, ls_) * _silu(cz_ref[b])
    is_b = lax.broadcasted_iota(jnp.int32, (acc_ref.shape[0], HEAD_DIM), 0) == b
    for hs in range(HEADS_PER_GROUP):
        cs = slice(hs * HEAD_DIM, (hs + 1) * HEAD_DIM)
        acc_ref[:, cs] = jnp.where(is_b, jnp.broadcast_to(yrow_ref[hs:hs + 1, :], is_b.shape),
                                   acc_ref[:, cs])

    @pl.when(b == pl.num_programs(0) - 1)
    def _():
        y_ref[...] = acc_ref[...].astype(BF16)


def _sample_attn(qk, h_s, caches, y_c, m_prompt, l, nb):
    r = SAMPLE_BLOCK
    qk3 = qk.reshape(r, 2 * N_HEADS, HEAD_DIM)
    v3 = h_s[:, COL_V:COL_V + W_QKV].reshape(r, N_HEADS, HEAD_DIM)
    cz3 = h_s[:, COL_C_Z:COL_C_Z + W_C].reshape(r, HEADS_PER_GROUP, HEAD_DIM)
    whole = lambda a: pl.BlockSpec(a.shape, lambda b, nd=a.ndim: (0,) * nd)
    cspec = pl.BlockSpec((None, None, SPAN, None, 2, HEADS_PER_GROUP, HEAD_DIM),
                         lambda b: (l, b, 0, 0, 0, 0, 0))
    return pl.pallas_call(
        _sample_attn_kernel,
        grid=(nb,),
        in_specs=[whole(qk3), whole(v3), whole(cz3), cspec, cspec, cspec,
                  pl.BlockSpec(memory_space=pl.ANY)],
        out_specs=pl.BlockSpec((r, W_C), lambda b: (m_prompt // r, 0)),
        out_shape=jax.ShapeDtypeStruct(y_c.shape, BF16),
        scratch_shapes=[pltpu.VMEM((r, W_C), F32), pltpu.VMEM((HEADS_PER_GROUP, HEAD_DIM), F32)],
        input_output_aliases={6: 0},
        compiler_params=_params("arbitrary"),
        name="sample_attn",
    )(qk3, v3, cz3, *caches, y_c)


def _kv_rows(k_rot, h, nb, s, g):
    keep = k_rot.shape[1]
    c0 = COL_V + g * W_C
    v = h[:nb * s, c0:c0 + W_C].reshape(nb, s, W_C)[:, -keep:]
    return jnp.stack([k_rot, v], axis=2).reshape(nb, keep, 2, HEADS_PER_GROUP, HEAD_DIM)


def _cache_view(c, dil):
    assert c.shape[2] == SPAN * dil
    return c.reshape(c.shape[0], c.shape[1], SPAN, dil, 2, HEADS_PER_GROUP, HEAD_DIM)


def kernel(x_prompt, x_sample, state_conv_a, state_conv_b, cache_kv_w128, cache_kv_w512,
           cache_kv_w2048, w_in, conv_a_w, conv_b_w, conv_b_bias, ln_b_g, ln_b_b, ln_d_g,
           ln_d_b, w_s, b_s, w_out_a, w_out_b, w_out_c, w_out_d, b_gate, w_o, ln_g, ln_b):
    bp, tp, _ = x_prompt.shape
    bs, ts, _ = x_sample.shape
    assert ts == 1 and bs <= SAMPLE_BLOCK
    mp = bp * tp
    assert mp % SAMPLE_BLOCK == 0
    m_all = mp + SAMPLE_BLOCK
    pad = SAMPLE_BLOCK - bs
    tiles = _tiles(m_all)

    w_o_b = w_o.astype(BF16)
    wouts = (w_out_a, w_out_b, w_out_c, w_out_d)
    caches = [_cache_view(c, d)
              for c, d in zip((cache_kv_w128, cache_kv_w512, cache_kv_w2048), DILATIONS)]

    cos_p, sin_p = _rope_tables(jnp.arange(tp, dtype=jnp.int32))
    cos_s, sin_s = _rope_tables(PAST_LEN + jnp.arange(ts, dtype=jnp.int32))

    x = jnp.concatenate([x_prompt.reshape(mp, D_MODEL), x_sample.reshape(bs, D_MODEL),
                         jnp.zeros((pad, D_MODEL), F32)], axis=0)
    xb = x.astype(BF16)

    ca_p, ca_s, cb_p, cb_s, dv_s = [], [], [], [], []
    kv_p = [[] for _ in range(N_GROUPS)]
    kv_s = [[] for _ in range(N_GROUPS)]
    y_a, y_b, y_c, y_d = (jnp.zeros((m_all, w), BF16) for w in (W_A, W_B, W_C, W_D))
    for l in range(DEPTH):
        h = _in_proj(xb, w_in, l, tiles["in_proj"], 512)
        y_a, st_a = _branch_a(h, bp, tp, conv_a_w, y_a, l, 512)
        y_b, st_b, w_gate_b = _branch_b(h, bp, tp, conv_b_w, conv_b_bias, ln_b_g, ln_b_b,
                                        w_in, y_b, l, 256)
        y_d, *wouts_b = _branch_d(h, bp, tp, ln_d_g, ln_d_b, w_s, b_s, wouts, y_d, l, 512)
        o0, l0, kr0 = _attn_group(h, bp, tp, cos_p, sin_p, 0, 8)
        o1, l1, kr1 = _attn_group(h, bp, tp, cos_p, sin_p, 1, 2)
        y_c, kr2 = _attn_group(h, bp, tp, cos_p, sin_p, 2, 1, others=(o0, l0, o1, l1),
                               y_init=y_c)
        ca_p.append(st_a)
        cb_p.append(st_b)
        for g, kr in enumerate((kr0, kr1, kr2)):
            kv_p[g].append(_kv_rows(kr, h, bp, tp, g))

        hist_a = jnp.pad(state_conv_a[l].transpose(1, 0, 2), ((0, 0), (0, pad), (0, 0)))
        hist_b = jnp.pad(state_conv_b[l].transpose(1, 0, 2), ((0, 0), (0, pad), (0, 0)))
        y_a, y_b, y_d, na, nb_, vn, qk_s = _sample_mix(
            h, mp, (y_a, y_b, y_d), hist_a, hist_b, conv_a_w, conv_b_w, conv_b_bias,
            ln_b_g, ln_b_b, ln_d_g, ln_d_b, w_s, b_s, cos_s, sin_s, l)
        h_s = h[mp:]
        y_c = _sample_attn(qk_s, h_s, caches, y_c, mp, l, bs)
        ca_s.append(na[:, :bs].transpose(1, 0, 2))
        cb_s.append(nb_[:, :bs].transpose(1, 0, 2))
        dv_s.append(vn[:bs].reshape(bs, ts, W_D))
        for g in range(N_GROUPS):
            c0 = g * W_C
            k = qk_s[:bs, W_QKV + c0:W_QKV + c0 + W_C]
            v = h_s[:bs, COL_V + c0:COL_V + c0 + W_C]
            kv_s[g].append(jnp.stack([k, v], axis=1).reshape(bs, ts, 2, HEADS_PER_GROUP, HEAD_DIM))

        merged = _gate_out(xb, w_gate_b, b_gate, (y_a, y_b, y_c, y_d), wouts_b, l,
                           tiles["gate_out"], 256)
        z = _proj_res(merged, w_o_b, x, l, tiles["proj_res"], 512)
        if l + 1 < DEPTH:
            x, xb = _ln_rows(z, ln_g, ln_b, l, tiles["ln_rows"])
        else:
            y_p, y_s = _ln_rows_split(z, ln_g, ln_b, l, mp, tiles["ln_out"])

    return (y_p.reshape(bp, tp, D_MODEL), y_s[:bs].reshape(bs, ts, D_MODEL),
            jnp.stack(ca_p), jnp.stack(ca_s), jnp.stack(cb_p), jnp.stack(cb_s),
            jnp.stack(kv_p[0]), jnp.stack(kv_s[0]), jnp.stack(kv_p[1]), jnp.stack(kv_s[1]),
            jnp.stack(kv_p[2]), jnp.stack(kv_s[2]), jnp.stack(dv_s))
```

```python
import functools

import jax
import jax.numpy as jnp
from jax import lax
from jax.experimental import pallas as pl
from jax.experimental.pallas import tpu as pltpu

D_MODEL = 4096
DEPTH = 2
PAST_LEN = 16384
W_A = 1024
W_B = 1024
HEAD_DIM = 128
N_GROUPS = 3
HEADS_PER_GROUP = 4
N_HEADS = N_GROUPS * HEADS_PER_GROUP
W_QKV = N_HEADS * HEAD_DIM
W_C = HEADS_PER_GROUP * HEAD_DIM
W_D = 1024
CHUNK = 128
D_GROUPS = 8
CONV_A = 3
CONV_B = 31
WINDOWS = (128, 512, 2048)
DILATIONS = (1, 4, 16)
SPAN = 128
ROPE_THETA = 10000.0
N_BRANCH = 4
LN_EPS = 1e-5
ALPHA = (2 * DEPTH) ** 0.25

COL_A_B, COL_A_C, COL_A_H, COL_A_Z = 0, 1024, 2048, 3072
COL_B_A, COL_B_G, COL_B_Z = 4096, 5120, 6144
COL_Q, COL_K, COL_V, COL_C_Z = 7168, 8704, 10240, 11776
COL_D_U, COL_D_V, COL_D_Z = 12288, 13312, 14336
N_BRANCH_COLS = 15360
COL_GATE = 15360

LANES = 128
SUBLANES = 8
SAMPLE_BLOCK = 64
VMEM_LIMIT_BYTES = 56 * 1024 * 1024

F32 = jnp.float32
BF16 = jnp.bfloat16


def _params(*sem):
    return pltpu.CompilerParams(dimension_semantics=sem, vmem_limit_bytes=VMEM_LIMIT_BYTES)


def _silu(v):
    return v * jax.nn.sigmoid(v)


def _layernorm(v, g, b):
    mu = jnp.mean(v, axis=-1, keepdims=True)
    c = v - mu
    var = jnp.mean(c * c, axis=-1, keepdims=True)
    return c * lax.rsqrt(var + LN_EPS) * g + b


def _row_param(p):
    return p.reshape(DEPTH, 1, p.shape[-1])


def _row_spec(l, c, nargs):
    if nargs == 1:
        return pl.BlockSpec((None, 1, c), lambda b: (l, 0, 0))
    return pl.BlockSpec((None, 1, c), lambda b, t: (l, 0, 0))


def _tiles(m_all):
    def largest(limit, mult):
        return max(t for t in range(mult, limit + 1, mult) if m_all % t == 0)
    m_prompt = m_all - SAMPLE_BLOCK
    return dict(in_proj=largest(2176, 16), gate_out=largest(768, 16),
                proj_res=largest(1536, 16), ln_rows=largest(384, 16),
                ln_out=max(t for t in (512, 256, 128, 64) if m_prompt % t == 0))


def _mm_kernel(x_ref, w_ref, o_ref):
    o_ref[...] = jnp.dot(x_ref[...], w_ref[...].astype(BF16), preferred_element_type=F32)


def _in_proj(xb, w_in, l, tm, tn):
    m = xb.shape[0]
    return pl.pallas_call(
        _mm_kernel,
        grid=(m // tm, N_BRANCH_COLS // tn),
        in_specs=[pl.BlockSpec((tm, D_MODEL), lambda i, j: (i, 0), pipeline_mode=pl.Buffered(1)),
                  pl.BlockSpec((None, D_MODEL, tn), lambda i, j: (l, 0, j))],
        out_specs=pl.BlockSpec((tm, tn), lambda i, j: (i, j)),
        out_shape=jax.ShapeDtypeStruct((m, N_BRANCH_COLS), F32),
        compiler_params=_params("parallel", "parallel"),
        name="in_proj",
    )(xb, w_in)


A_PAD = 8


def _branch_a_kernel(ab_ref, ac_ref, ah_ref, az_ref, w_ref, y_init, y_ref, st_ref, ext_ref):
    del y_init
    t = pl.program_id(1)
    tt = ab_ref.shape[0]

    @pl.when(t == 0)
    def _():
        ext_ref[0:A_PAD, :] = jnp.zeros((A_PAD, W_A), F32)

    @pl.when(t > 0)
    def _():
        ext_ref[0:A_PAD, :] = ext_ref[tt:tt + A_PAD, :]

    u = ac_ref[...] * ah_ref[...]
    ext_ref[A_PAD:A_PAD + tt, :] = u
    w = w_ref[...]
    conv = (w[0:1, :] * ext_ref[A_PAD - 2:A_PAD - 2 + tt, :]
            + w[1:2, :] * ext_ref[A_PAD - 1:A_PAD - 1 + tt, :]
            + w[2:3, :] * u)
    y_ref[...] = (ab_ref[...] * conv * _silu(az_ref[...])).astype(BF16)
    st_ref[...] = ext_ref[A_PAD + tt - (CONV_A - 1):A_PAD + tt, :]


ANY_SPEC = pl.BlockSpec(memory_space=pl.ANY)


def _branch_a(h, nb, s, conv_a_w, y_init, l, tt):
    nt = s // tt
    col = lambda c: pl.BlockSpec((tt, W_A), lambda b, t: (b * nt + t, c // W_A))
    return pl.pallas_call(
        _branch_a_kernel,
        grid=(nb, nt),
        in_specs=[col(COL_A_B), col(COL_A_C), col(COL_A_H), col(COL_A_Z),
                  pl.BlockSpec((None, CONV_A, W_A), lambda b, t: (l, 0, 0)), ANY_SPEC],
        input_output_aliases={5: 0},
        out_specs=[pl.BlockSpec((tt, W_A), lambda b, t: (b * nt + t, 0)),
                   pl.BlockSpec((None, CONV_A - 1, W_A), lambda b, t: (b, 0, 0))],
        out_shape=[jax.ShapeDtypeStruct((h.shape[0], W_A), BF16),
                   jax.ShapeDtypeStruct((nb, CONV_A - 1, W_A), F32)],
        scratch_shapes=[pltpu.VMEM((A_PAD + tt, W_A), F32)],
        compiler_params=_params("parallel", "arbitrary"),
        name="branch_a",
    )(h, h, h, h, conv_a_w, y_init)


B_PAD = 32
B_ROWS = 64


def _branch_b_kernel(ba_ref, bg_ref, bz_ref, w_ref, bias_ref, g_ref, b_ref, wg_ref, y_init,
                     y_ref, st_ref, wgb_ref, ext_ref, sh_ref, cb_ref):
    del y_init
    t = pl.program_id(1)
    tt = ba_ref.shape[0]
    wgb_ref[...] = wg_ref[...].astype(BF16)

    @pl.when(t == 0)
    def _():
        ext_ref[0:B_PAD, :] = jnp.zeros((B_PAD, W_B), F32)

    @pl.when(t > 0)
    def _():
        ext_ref[0:B_PAD, :] = ext_ref[tt:tt + B_PAD, :]

    ext_ref[B_PAD:B_PAD + tt, :] = ba_ref[...] * jax.nn.sigmoid(bg_ref[...])
    sh_rows = sh_ref.shape[1]
    for s in range(1, SUBLANES):
        sh_ref[s - 1] = ext_ref[s:s + sh_rows, :]
    first = B_PAD - (CONV_B - 1)
    for c in range(W_B // LANES):
        cs = slice(c * LANES, (c + 1) * LANES)
        wc = w_ref[:, cs]
        for rc in range(tt // B_ROWS):
            acc = jnp.zeros((B_ROWS, LANES), F32)
            for k in range(CONV_B):
                q, s = divmod(first + k, SUBLANES)
                r0 = SUBLANES * q + rc * B_ROWS
                if s == 0:
                    tap = ext_ref[r0:r0 + B_ROWS, cs]
                else:
                    tap = sh_ref[s - 1, r0:r0 + B_ROWS, cs]
                acc = acc + wc[k:k + 1, :] * tap
            cb_ref[rc * B_ROWS:(rc + 1) * B_ROWS, cs] = acc
    cb = cb_ref[...] + bias_ref[...]
    ln = _layernorm(cb, g_ref[...], b_ref[...])
    y_ref[...] = (_silu(ln) * _silu(bz_ref[...])).astype(BF16)
    st_ref[...] = ext_ref[B_PAD + tt - (CONV_B - 1):B_PAD + tt, :]


def _branch_b(h, nb, s, conv_b_w, conv_b_bias, ln_b_g, ln_b_b, w_in, y_init, l, tt):
    nt = s // tt
    n_gate = N_BRANCH * D_MODEL
    wcols = n_gate // (nb * nt)
    assert wcols % LANES == 0 and COL_GATE % wcols == 0
    col = lambda c: pl.BlockSpec((tt, W_B), lambda b, t: (b * nt + t, c // W_B))
    return pl.pallas_call(
        _branch_b_kernel,
        grid=(nb, nt),
        in_specs=[col(COL_B_A), col(COL_B_G), col(COL_B_Z),
                  pl.BlockSpec((None, CONV_B, W_B), lambda b, t: (l, 0, 0)),
                  _row_spec(l, W_B, 2), _row_spec(l, W_B, 2), _row_spec(l, W_B, 2),
                  pl.BlockSpec((None, D_MODEL, wcols),
                               lambda b, t: (l, 0, COL_GATE // wcols + b * nt + t)), ANY_SPEC],
        input_output_aliases={8: 0},
        out_specs=[pl.BlockSpec((tt, W_B), lambda b, t: (b * nt + t, 0)),
                   pl.BlockSpec((None, CONV_B - 1, W_B), lambda b, t: (b, 0, 0)),
                   pl.BlockSpec((D_MODEL, wcols), lambda b, t: (0, b * nt + t))],
        out_shape=[jax.ShapeDtypeStruct((h.shape[0], W_B), BF16),
                   jax.ShapeDtypeStruct((nb, CONV_B - 1, W_B), F32),
                   jax.ShapeDtypeStruct((D_MODEL, n_gate), BF16)],
        scratch_shapes=[pltpu.VMEM((B_PAD + tt, W_B), F32),
                        pltpu.VMEM((SUBLANES - 1, B_PAD + tt - SUBLANES, W_B), F32),
                        pltpu.VMEM((tt, W_B), F32)],
        compiler_params=_params("parallel", "arbitrary"),
        name="branch_b",
    )(h, h, h, conv_b_w, _row_param(conv_b_bias), _row_param(ln_b_g), _row_param(ln_b_b), w_in,
      y_init)


def _branch_d_kernel(du_ref, dv_ref, dz_ref, g_ref, b_ref, ws_ref, bs_ref,
                     wa_ref, wb_ref, wc_ref, wd_ref, y_init,
                     y_ref, wab_ref, wbb_ref, wcb_ref, wdb_ref):
    del y_init
    tt = du_ref.shape[0]
    for src, dst in ((wa_ref, wab_ref), (wb_ref, wbb_ref), (wc_ref, wcb_ref), (wd_ref, wdb_ref)):
        dst[...] = src[...].astype(BF16)
    vb = _layernorm(dv_ref[...], g_ref[...], b_ref[...]).astype(BF16)
    row = lax.broadcasted_iota(jnp.int32, (CHUNK, CHUNK), 0)
    col = lax.broadcasted_iota(jnp.int32, (CHUNK, CHUNK), 1)
    causal = col <= row
    for g in range(D_GROUPS):
        cs = slice(g * LANES, (g + 1) * LANES)
        wm = jnp.where(causal, ws_ref[g], 0.0).astype(BF16)
        for c in range(tt // CHUNK):
            rs = slice(c * CHUNK, (c + 1) * CHUNK)
            mix = jnp.dot(wm, vb[rs, cs], preferred_element_type=F32) + bs_ref[:, cs]
            y_ref[rs, cs] = (du_ref[rs, cs] * mix * _silu(dz_ref[rs, cs])).astype(BF16)


def _bias_slab(b_s_l):
    return jnp.repeat(b_s_l.T, LANES, axis=1)


def _branch_d(h, nb, s, ln_d_g, ln_d_b, w_s, b_s, wouts, y_init, l, tt):
    nt = s // tt
    steps = nb * nt
    col = lambda c: pl.BlockSpec((tt, W_D), lambda b, t: (b * nt + t, c // W_D))
    w_in_specs, w_out_specs, w_out_shapes = [], [], []
    for w in wouts:
        rows = w.shape[1] // steps
        assert rows % 16 == 0 and rows * steps == w.shape[1]
        w_in_specs.append(pl.BlockSpec((None, rows, D_MODEL), lambda b, t: (l, b * nt + t, 0)))
        w_out_specs.append(pl.BlockSpec((rows, D_MODEL), lambda b, t: (b * nt + t, 0)))
        w_out_shapes.append(jax.ShapeDtypeStruct((w.shape[1], D_MODEL), BF16))
    return pl.pallas_call(
        _branch_d_kernel,
        grid=(nb, nt),
        in_specs=[col(COL_D_U), col(COL_D_V), col(COL_D_Z),
                  _row_spec(l, W_D, 2), _row_spec(l, W_D, 2),
                  pl.BlockSpec((None, D_GROUPS, CHUNK, CHUNK), lambda b, t: (l, 0, 0, 0)),
                  pl.BlockSpec((CHUNK, W_D), lambda b, t: (0, 0))] + w_in_specs + [ANY_SPEC],
        input_output_aliases={7 + len(wouts): 0},
        out_specs=[pl.BlockSpec((tt, W_D), lambda b, t: (b * nt + t, 0))] + w_out_specs,
        out_shape=[jax.ShapeDtypeStruct((h.shape[0], W_D), BF16)] + w_out_shapes,
        compiler_params=_params("parallel", "parallel"),
        name="branch_d",
    )(h, h, h, _row_param(ln_d_g), _row_param(ln_d_b), w_s, _bias_slab(b_s[l]), *wouts, y_init)


def _rope_tables(pos):
    half = HEAD_DIM // 2
    inv = ROPE_THETA ** (-jnp.arange(half, dtype=F32) / half)
    ang = pos.astype(F32)[:, None] * inv[None, :]
    cos = jnp.cos(ang)
    sin = jnp.sin(ang)
    return jnp.concatenate([cos, cos], axis=-1), jnp.concatenate([-sin, sin], axis=-1)


def _rope(v, cos, sin):
    return v * cos + pltpu.roll(v, HEAD_DIM // 2, v.ndim - 1) * sin


def _merge_groups(os_, ls_):
    mx = jnp.maximum(jnp.maximum(ls_[0], ls_[1]), ls_[2])
    es = [jnp.exp(l_ - mx) for l_ in ls_]
    den = es[0] + es[1] + es[2]
    return (es[0] / den) * os_[0] + (es[1] / den) * os_[1] + (es[2] / den) * os_[2]


ATTN_ILP = 8


PERM_MIN_DIL = SUBLANES


def _attn_kernel(*refs, dil, blocks, use_prev, keep, merge, perm):
    refs = list(refs)
    q_ref, kc_ref, vc_ref, cos_ref, sin_ref = refs[:5]
    del refs[:5]
    if use_prev:
        kp_ref, vp_ref, cosp_ref, sinp_ref = refs[:4]
        del refs[:4]
    if perm:
        p_ref = refs.pop(0)
    if merge:
        oa_ref, la_ref, ob_ref, lb_ref, cz_ref, _y_init, y_ref = refs[:7]
        del refs[:7]
    else:
        o_ref, l_ref = refs[:2]
        del refs[:2]
    kr_ref, qs_ref, ks_ref = refs[:3]
    del refs[:3]
    if use_prev:
        kps_ref = refs.pop(0)
    if perm:
        qd_ref, kd_ref, vd_ref = refs[:3]
        del refs[:3]
        if use_prev:
            kpd_ref, vpd_ref = refs[:2]
            del refs[:2]
    if merge:
        o_ref, l_ref = refs[:2]
        del refs[:2]

    n = pl.program_id(2)
    tile = q_ref.shape[0]
    scale = HEAD_DIM ** -0.5
    cos = cos_ref[...]
    sin = sin_ref[...]
    qs_ref[...] = _rope(q_ref[...], cos, sin)
    ks_ref[...] = _rope(kc_ref[...], cos, sin)
    kr_ref[...] = ks_ref[tile - keep:tile, :]
    if use_prev:
        kps_ref[...] = _rope(kp_ref[...], cosp_ref[...], sinp_ref[...])
    if perm:
        pmat = p_ref[...]

        def deinterleave(src_ref, dst_ref):
            for s0 in range(0, src_ref.shape[0], SPAN):
                dst_ref[s0:s0 + SPAN, :] = jnp.dot(pmat, src_ref[s0:s0 + SPAN, :].astype(BF16),
                                                   preferred_element_type=F32)

        deinterleave(qs_ref, qd_ref)
        deinterleave(ks_ref, kd_ref)
        deinterleave(vc_ref, vd_ref)
        if use_prev:
            deinterleave(kps_ref, kpd_ref)
            deinterleave(vp_ref, vpd_ref)

    width = 2 * SPAN if use_prev else SPAN
    row = lax.broadcasted_iota(jnp.int32, (SPAN, width), 0)
    col = lax.broadcasted_iota(jnp.int32, (SPAN, width), 1)
    if use_prev:
        band = jnp.logical_or(jnp.logical_and(col < SPAN, col >= row),
                              jnp.logical_and(col >= SPAN, col - SPAN <= row))
        has_prev = jnp.zeros((SPAN, width), jnp.int32) + n > 0
        first = jnp.logical_and(band, jnp.logical_or(col >= SPAN, has_prev))
    else:
        band = col <= row
        first = band
    nt = (((1,), (1,)), ((), ()))

    def rows(j, r, count=SPAN):
        start = j * SPAN * dil + r
        return pl.ds(start, count) if dil == 1 else pl.ds(start, count, stride=dil)

    piece = SPAN // dil

    def take(ref, slab0, n_slabs, r):
        if not perm:
            return ref[rows(slab0 // dil, r, n_slabs // dil * SPAN), :]
        return jnp.concatenate(
            [ref[(slab0 + c) * SPAN + r * piece:(slab0 + c) * SPAN + (r + 1) * piece, :]
             for c in range(n_slabs)], axis=0)

    q_src, k_src, v_src = (qd_ref, kd_ref, vd_ref) if perm else (qs_ref, ks_ref, vc_ref)
    if use_prev:
        kp_src, vp_src = (kpd_ref, vpd_ref) if perm else (kps_ref, vp_ref)

    def load_kv(j, r):
        if not use_prev:
            return (take(k_src, j * dil, dil, r).astype(BF16),
                    take(v_src, j * dil, dil, r).astype(BF16))
        if j > 0:
            return (take(k_src, (j - 1) * dil, 2 * dil, r).astype(BF16),
                    take(v_src, (j - 1) * dil, 2 * dil, r).astype(BF16))
        k2 = jnp.concatenate([take(kp_src, 0, dil, r), take(k_src, 0, dil, r)], axis=0)
        v2 = jnp.concatenate([take(vp_src, 0, dil, r), take(v_src, 0, dil, r)], axis=0)
        return k2.astype(BF16), v2.astype(BF16)

    def run(grp):
        qs = [take(q_src, j * dil, dil, r).astype(BF16) for r, j in grp]
        kvs = [load_kv(j, r) for r, j in grp]
        ss = [lax.dot_general(q, k2, nt, preferred_element_type=F32) * scale
              for q, (k2, _) in zip(qs, kvs)]
        ss = [jnp.where(first if j == 0 else band, s, -jnp.inf) for s, (r, j) in zip(ss, grp)]
        mxs = [jnp.max(s, axis=1, keepdims=True) for s in ss]
        ps = [jnp.exp(s - mx) for s, mx in zip(ss, mxs)]
        dens = [jnp.sum(p, axis=1, keepdims=True) for p in ps]
        outs = [jnp.dot((p / den).astype(BF16), v2, preferred_element_type=F32)
                for p, den, (_, v2) in zip(ps, dens, kvs)]
        for (r, j), o, mx, den in zip(grp, outs, mxs, dens):
            idx = rows(j, r)
            o_ref[idx, :] = o
            l_ref[idx, :] = jnp.broadcast_to(mx + jnp.log(den), (SPAN, HEAD_DIM))

    order = [(r, j) for r in range(dil) for j in range(blocks)]
    for i in range(0, len(order), ATTN_ILP):
        run(order[i:i + ATTN_ILP])
    if merge:
        om = _merge_groups([oa_ref[...], ob_ref[...], o_ref[...]],
                           [la_ref[...], lb_ref[...], l_ref[...]])
        y_ref[...] = (om * _silu(cz_ref[...])).astype(BF16)


def _slab_permutation(dil):
    piece = SPAN // dil
    out_row = jnp.arange(SPAN)
    src = (out_row % piece) * dil + out_row // piece
    return (src[:, None] == jnp.arange(SPAN)[None, :]).astype(BF16)


def _attn_group(h, nb, s, cos, sin, g, blocks, others=None, y_init=None):
    dil = DILATIONS[g]
    unit = SPAN * dil
    tile = unit * blocks
    use_prev = s > unit
    per = tile // unit
    nt, nu = s // tile, s // unit
    keep = min(WINDOWS[g], s)
    assert keep <= tile and s % tile == 0
    merge = others is not None
    aliases = {}
    head = lambda c0: (lambda hs: c0 // HEAD_DIM + g * HEADS_PER_GROUP + hs)
    cq, ck, cv = head(COL_Q), head(COL_K), head(COL_V)
    prev = lambda n: jnp.maximum(n * per - 1, 0)
    col_blk = lambda rows_, fn: pl.BlockSpec((rows_, HEAD_DIM), fn)
    own = lambda c: col_blk(tile, lambda b, hs, n: (b * nt + n, c(hs)))
    in_specs = [own(cq), own(ck), own(cv),
                col_blk(tile, lambda b, hs, n: (n, 0)), col_blk(tile, lambda b, hs, n: (n, 0))]
    args = [h, h, h, cos, sin]
    if use_prev:
        before = lambda c: col_blk(unit, lambda b, hs, n: (b * nu + prev(n), c(hs)))
        in_specs += [before(ck), before(cv),
                     col_blk(unit, lambda b, hs, n: (prev(n), 0)),
                     col_blk(unit, lambda b, hs, n: (prev(n), 0))]
        args += [h, h, cos, sin]
    perm = dil >= PERM_MIN_DIL
    if perm:
        in_specs.append(pl.BlockSpec((SPAN, SPAN), lambda b, hs, n: (0, 0)))
        args.append(_slab_permutation(dil))
    plain = col_blk(tile, lambda b, hs, n: (b * nt + n, hs))
    if merge:
        in_specs += [plain] * 4 + [own(lambda hs: COL_C_Z // HEAD_DIM + hs), ANY_SPEC]
        args += list(others) + [h, y_init]
        aliases = {len(args) - 1: 0}
        out_specs = [plain]
        out_shape = [jax.ShapeDtypeStruct((h.shape[0], W_C), BF16)]
    else:
        out_specs = [plain, plain]
        out_shape = [jax.ShapeDtypeStruct((nb * s, W_C), F32)] * 2
    out_specs.append(pl.BlockSpec((None, keep, HEAD_DIM), lambda b, hs, n: (b, 0, hs)))
    out_shape.append(jax.ShapeDtypeStruct((nb, keep, W_C), F32))
    buf = lambda rows_: pltpu.VMEM((rows_, HEAD_DIM), F32)
    scratch = [buf(tile), buf(tile)]
    if use_prev:
        scratch.append(buf(unit))
    if perm:
        scratch += [buf(tile)] * 3
        if use_prev:
            scratch += [buf(unit)] * 2
    if merge:
        scratch += [buf(tile)] * 2
    kern = functools.partial(_attn_kernel, dil=dil, blocks=blocks, use_prev=use_prev,
                             keep=keep, merge=merge, perm=perm)
    return pl.pallas_call(
        kern,
        grid=(nb, HEADS_PER_GROUP, nt),
        in_specs=in_specs,
        out_specs=out_specs,
        out_shape=out_shape,
        scratch_shapes=scratch,
        input_output_aliases=aliases,
        compiler_params=_params("parallel", "parallel", "arbitrary"),
        name=f"attn_g{g}",
    )(*args)


WO_CAST_ROWS = 128


def _gate_out_kernel(x_ref, wg0, wg1, wg2, wg3, bg0, bg1, bg2, bg3,
                     ya, yb, yc, yd, woa, wob, woc, wod, wo_ref, o_ref, wo_b_ref, *, n_cast):
    step = pl.program_id(0) * pl.num_programs(1) + pl.program_id(1)

    @pl.when(step < n_cast)
    def _():
        wo_b_ref[...] = wo_ref[...].astype(BF16)

    x = x_ref[...]
    acc = None
    for wg, bg, y, wo in ((wg0, bg0, ya, woa), (wg1, bg1, yb, wob),
                          (wg2, bg2, yc, woc), (wg3, bg3, yd, wod)):
        gate = jax.nn.sigmoid(jnp.dot(x, wg[...], preferred_element_type=F32) + bg[...])
        part = jnp.dot(y[...], wo[...], preferred_element_type=F32)
        acc = gate * part if acc is None else acc + gate * part
    o_ref[...] = acc.astype(BF16)


def _gate_out(xb, w_gate_b, b_gate, ys, wouts_b, w_o, l, tm, tn):
    m = xb.shape[0]
    nj = D_MODEL // tn
    n_cast = D_MODEL // WO_CAST_ROWS
    assert (m // tm) * nj >= n_cast
    cast_blk = lambda i, j: jnp.minimum(i * nj + j, n_cast - 1)
    wg_spec = lambda br: pl.BlockSpec((D_MODEL, tn), lambda i, j: (0, br * nj + j))
    bg_spec = lambda br: pl.BlockSpec((None, 1, tn), lambda i, j: (l, 0, br * nj + j))
    y_spec = lambda y: pl.BlockSpec((tm, y.shape[1]), lambda i, j: (i, 0))
    wo_spec = lambda w: pl.BlockSpec((w.shape[0], tn), lambda i, j: (0, j))
    bg3 = b_gate.reshape(DEPTH, 1, N_BRANCH * D_MODEL)
    return pl.pallas_call(
        functools.partial(_gate_out_kernel, n_cast=n_cast),
        grid=(m // tm, nj),
        in_specs=([pl.BlockSpec((tm, D_MODEL), lambda i, j: (i, 0))]
                  + [wg_spec(br) for br in range(N_BRANCH)]
                  + [bg_spec(br) for br in range(N_BRANCH)]
                  + [y_spec(y) for y in ys]
                  + [wo_spec(w) for w in wouts_b]
                  + [pl.BlockSpec((None, WO_CAST_ROWS, D_MODEL),
                                  lambda i, j: (l, cast_blk(i, j), 0))]),
        out_specs=[pl.BlockSpec((tm, tn), lambda i, j: (i, j)),
                   pl.BlockSpec((WO_CAST_ROWS, D_MODEL), lambda i, j: (cast_blk(i, j), 0))],
        out_shape=[jax.ShapeDtypeStruct((m, D_MODEL), BF16),
                   jax.ShapeDtypeStruct((D_MODEL, D_MODEL), BF16)],
        compiler_params=_params("arbitrary", "arbitrary"),
        name="gate_out",
    )(xb, *([w_gate_b] * N_BRANCH), *([bg3] * N_BRANCH), *ys, *wouts_b, w_o)


def _proj_res_kernel(m_ref, wo_ref, x_ref, z_ref):
    z_ref[...] = ALPHA * x_ref[...] + jnp.dot(m_ref[...], wo_ref[...], preferred_element_type=F32)


def _proj_res(merged, w_o_b, x, tm, tn):
    m = x.shape[0]
    return pl.pallas_call(
        _proj_res_kernel,
        grid=(m // tm, D_MODEL // tn),
        in_specs=[pl.BlockSpec((tm, D_MODEL), lambda i, j: (i, 0)),
                  pl.BlockSpec((D_MODEL, tn), lambda i, j: (0, j)),
                  pl.BlockSpec((tm, tn), lambda i, j: (i, j))],
        out_specs=pl.BlockSpec((tm, tn), lambda i, j: (i, j)),
        out_shape=jax.ShapeDtypeStruct((m, D_MODEL), F32),
        compiler_params=_params("parallel", "parallel"),
        name="proj_res",
    )(merged, w_o_b, x)


def _ln_rows_kernel(z_ref, g_ref, b_ref, y_ref, yb_ref):
    y = _layernorm(z_ref[...], g_ref[...], b_ref[...])
    y_ref[...] = y
    yb_ref[...] = y.astype(BF16)


def _ln_rows(z, ln_g, ln_b, l, tr):
    m = z.shape[0]
    blk = pl.BlockSpec((tr, D_MODEL), lambda i: (i, 0))
    return pl.pallas_call(
        _ln_rows_kernel,
        grid=(m // tr,),
        in_specs=[blk, _row_spec(l, D_MODEL, 1), _row_spec(l, D_MODEL, 1)],
        out_specs=[blk, blk],
        out_shape=[jax.ShapeDtypeStruct((m, D_MODEL), F32),
                   jax.ShapeDtypeStruct((m, D_MODEL), BF16)],
        compiler_params=_params("parallel"),
        name="ln_rows",
    )(z, _row_param(ln_g), _row_param(ln_b))


def _ln_f32_kernel(z_ref, g_ref, b_ref, y_ref):
    y_ref[...] = _layernorm(z_ref[...], g_ref[...], b_ref[...])


def _ln_rows_split(z, ln_g, ln_b, l, m_prompt, tr):
    assert z.shape[0] == m_prompt + SAMPLE_BLOCK and m_prompt % tr == 0

    def part(rows, tile, first_block):
        return pl.pallas_call(
            _ln_f32_kernel,
            grid=(rows // tile,),
            in_specs=[pl.BlockSpec((tile, D_MODEL), lambda i: (first_block + i, 0)),
                      _row_spec(l, D_MODEL, 1), _row_spec(l, D_MODEL, 1)],
            out_specs=pl.BlockSpec((tile, D_MODEL), lambda i: (i, 0)),
            out_shape=jax.ShapeDtypeStruct((rows, D_MODEL), F32),
            compiler_params=_params("parallel"),
            name="ln_rows_out",
        )(z, _row_param(ln_g), _row_param(ln_b))

    return part(m_prompt, tr, 0), part(SAMPLE_BLOCK, SAMPLE_BLOCK, m_prompt // SAMPLE_BLOCK)


def _sample_mix_kernel(h_ref, ha_ref, hb_ref, wa_ref, wb_ref, bias_ref, lbg_ref, lbb_ref,
                       ldg_ref, ldb_ref, ws0_ref, bs0_ref, cos_ref, sin_ref,
                       ya_in, yb_in, yd_in,
                       ya_ref, yb_ref, yd_ref, na_ref, nb_ref, vn_ref, qk_ref):
    del ya_in, yb_in, yd_in
    col = lambda c, w: h_ref[:, c:c + w]
    u_a = col(COL_A_C, W_A) * col(COL_A_H, W_A)
    conv_a = wa_ref[0:1, :] * ha_ref[0] + wa_ref[1:2, :] * ha_ref[1] + wa_ref[2:3, :] * u_a
    ya_ref[...] = (col(COL_A_B, W_A) * conv_a * _silu(col(COL_A_Z, W_A))).astype(BF16)
    na_ref[0] = ha_ref[1]
    na_ref[1] = u_a
    u_b = col(COL_B_A, W_B) * jax.nn.sigmoid(col(COL_B_G, W_B))
    cb = wb_ref[CONV_B - 1:CONV_B, :] * u_b + bias_ref[...]
    for k in range(CONV_B - 1):
        cb = cb + wb_ref[k:k + 1, :] * hb_ref[k]
    ln_b = _layernorm(cb, lbg_ref[...], lbb_ref[...])
    yb_ref[...] = (_silu(ln_b) * _silu(col(COL_B_Z, W_B))).astype(BF16)
    for k in range(CONV_B - 2):
        nb_ref[k] = hb_ref[k + 1]
    nb_ref[CONV_B - 2] = u_b
    vn = _layernorm(col(COL_D_V, W_D), ldg_ref[...], ldb_ref[...])
    vn_ref[...] = vn
    mix = ws0_ref[...] * vn + bs0_ref[...]
    yd_ref[...] = (col(COL_D_U, W_D) * mix * _silu(col(COL_D_Z, W_D))).astype(BF16)
    cos = cos_ref[...]
    sin = sin_ref[...]
    for hh in range(2 * N_HEADS):
        cs = slice(hh * HEAD_DIM, (hh + 1) * HEAD_DIM)
        qk_ref[:, cs] = _rope(h_ref[:, COL_Q + hh * HEAD_DIM:COL_Q + (hh + 1) * HEAD_DIM], cos, sin)


def _sample_mix(h, m_prompt, ys, hist_a, hist_b, conv_a_w, conv_b_w, conv_b_bias, ln_b_g, ln_b_b,
                ln_d_g, ln_d_b, w_s, b_s, cos, sin, l):
    r = SAMPLE_BLOCK
    blk = m_prompt // r
    ws0 = jnp.repeat(w_s[l, :, 0, 0], LANES)[None, :]
    bs0 = jnp.repeat(b_s[l, :, 0], LANES)[None, :]
    row = lambda p: p[l][None, :]
    small = [hist_a, hist_b, conv_a_w[l], conv_b_w[l], row(conv_b_bias), row(ln_b_g), row(ln_b_b),
             row(ln_d_g), row(ln_d_b), ws0, bs0, cos, sin]
    whole = lambda a: pl.BlockSpec(a.shape, lambda i, nd=a.ndim: (0,) * nd)
    y_spec = lambda w: pl.BlockSpec((r, w), lambda i: (blk, 0))
    f = lambda shape: jax.ShapeDtypeStruct(shape, F32)
    n_in = 1 + len(small)
    return pl.pallas_call(
        _sample_mix_kernel,
        grid=(1,),
        in_specs=([pl.BlockSpec((r, N_BRANCH_COLS), lambda i: (blk, 0))]
                  + [whole(a) for a in small]
                  + [pl.BlockSpec(memory_space=pl.ANY)] * 3),
        out_specs=[y_spec(W_A), y_spec(W_B), y_spec(W_D),
                   whole(f((CONV_A - 1, r, W_A))), whole(f((CONV_B - 1, r, W_B))),
                   whole(f((r, W_D))), whole(f((r, 2 * W_QKV)))],
        out_shape=[jax.ShapeDtypeStruct(ys[0].shape, BF16), jax.ShapeDtypeStruct(ys[1].shape, BF16),
                   jax.ShapeDtypeStruct(ys[2].shape, BF16),
                   f((CONV_A - 1, r, W_A)), f((CONV_B - 1, r, W_B)), f((r, W_D)), f((r, 2 * W_QKV))],
        input_output_aliases={n_in: 0, n_in + 1: 1, n_in + 2: 2},
        compiler_params=_params("arbitrary"),
        name="sample_mix",
    )(h, *small, *ys)


def _sample_attn_kernel(qk_ref, v_ref, cz_ref, c0_ref, c1_ref, c2_ref, yc_in, y_ref,
                        acc_ref, yrow_ref):
    del yc_in
    b = pl.program_id(0)
    scale = HEAD_DIM ** -0.5

    @pl.when(b == 0)
    def _():
        acc_ref[...] = jnp.zeros(acc_ref.shape, F32)

    os_, ls_ = [], []
    for g, c_ref in enumerate((c0_ref, c1_ref, c2_ref)):
        q = qk_ref[b, pl.ds(g * HEADS_PER_GROUP, HEADS_PER_GROUP), :]
        k_new = qk_ref[b, pl.ds(N_HEADS + g * HEADS_PER_GROUP, HEADS_PER_GROUP), :]
        v_new = v_ref[b, pl.ds(g * HEADS_PER_GROUP, HEADS_PER_GROUP), :]
        kc = c_ref[:, 0]
        vc = c_ref[:, 1]
        s_c = jnp.sum(kc * q[None], axis=-1, keepdims=True) * scale
        s_n = jnp.sum(k_new * q, axis=-1, keepdims=True) * scale
        mx = jnp.maximum(jnp.max(s_c, axis=0), s_n)
        p_c = jnp.exp(s_c - mx[None])
        p_n = jnp.exp(s_n - mx)
        den = jnp.sum(p_c, axis=0) + p_n
        os_.append((jnp.sum(p_c * vc, axis=0) + p_n * v_new) / den)
        ls_.append(mx + jnp.log(den))
    yrow_ref[...] = _merge_groups(os_, ls_) * _silu(cz_ref[b])
    is_b = lax.broadcasted_iota(jnp.int32, (acc_ref.shape[0], HEAD_DIM), 0) == b
    for hs in range(HEADS_PER_GROUP):
        cs = slice(hs * HEAD_DIM, (hs + 1) * HEAD_DIM)
        acc_ref[:, cs] = jnp.where(is_b, jnp.broadcast_to(yrow_ref[hs:hs + 1, :], is_b.shape),
                                   acc_ref[:, cs])

    @pl.when(b == pl.num_programs(0) - 1)
    def _():
        y_ref[...] = acc_ref[...].astype(BF16)


def _sample_attn(qk, h_s, caches, y_c, m_prompt, l, nb):
    r = SAMPLE_BLOCK
    qk3 = qk.reshape(r, 2 * N_HEADS, HEAD_DIM)
    v3 = h_s[:, COL_V:COL_V + W_QKV].reshape(r, N_HEADS, HEAD_DIM)
    cz3 = h_s[:, COL_C_Z:COL_C_Z + W_C].reshape(r, HEADS_PER_GROUP, HEAD_DIM)
    whole = lambda a: pl.BlockSpec(a.shape, lambda b, nd=a.ndim: (0,) * nd)
    cspec = pl.BlockSpec((None, None, SPAN, None, 2, HEADS_PER_GROUP, HEAD_DIM),
                         lambda b: (l, b, 0, 0, 0, 0, 0))
    return pl.pallas_call(
        _sample_attn_kernel,
        grid=(nb,),
        in_specs=[whole(qk3), whole(v3), whole(cz3), cspec, cspec, cspec,
                  pl.BlockSpec(memory_space=pl.ANY)],
        out_specs=pl.BlockSpec((r, W_C), lambda b: (m_prompt // r, 0)),
        out_shape=jax.ShapeDtypeStruct(y_c.shape, BF16),
        scratch_shapes=[pltpu.VMEM((r, W_C), F32), pltpu.VMEM((HEADS_PER_GROUP, HEAD_DIM), F32)],
        input_output_aliases={6: 0},
        compiler_params=_params("arbitrary"),
        name="sample_attn",
    )(qk3, v3, cz3, *caches, y_c)


def _kv_rows(k_rot, h, nb, s, g):
    keep = k_rot.shape[1]
    c0 = COL_V + g * W_C
    v = h[:nb * s, c0:c0 + W_C].reshape(nb, s, W_C)[:, -keep:]
    return jnp.stack([k_rot, v], axis=2).reshape(nb, keep, 2, HEADS_PER_GROUP, HEAD_DIM)


def _cache_view(c, dil):
    assert c.shape[2] == SPAN * dil
    return c.reshape(c.shape[0], c.shape[1], SPAN, dil, 2, HEADS_PER_GROUP, HEAD_DIM)


def kernel(x_prompt, x_sample, state_conv_a, state_conv_b, cache_kv_w128, cache_kv_w512,
           cache_kv_w2048, w_in, conv_a_w, conv_b_w, conv_b_bias, ln_b_g, ln_b_b, ln_d_g,
           ln_d_b, w_s, b_s, w_out_a, w_out_b, w_out_c, w_out_d, b_gate, w_o, ln_g, ln_b):
    bp, tp, _ = x_prompt.shape
    bs, ts, _ = x_sample.shape
    assert ts == 1 and bs <= SAMPLE_BLOCK
    mp = bp * tp
    assert mp % SAMPLE_BLOCK == 0
    m_all = mp + SAMPLE_BLOCK
    pad = SAMPLE_BLOCK - bs
    tiles = _tiles(m_all)

    wouts = (w_out_a, w_out_b, w_out_c, w_out_d)
    caches = [_cache_view(c, d)
              for c, d in zip((cache_kv_w128, cache_kv_w512, cache_kv_w2048), DILATIONS)]

    cos_p, sin_p = _rope_tables(jnp.arange(tp, dtype=jnp.int32))
    cos_s, sin_s = _rope_tables(PAST_LEN + jnp.arange(ts, dtype=jnp.int32))

    x = jnp.concatenate([x_prompt.reshape(mp, D_MODEL), x_sample.reshape(bs, D_MODEL),
                         jnp.zeros((pad, D_MODEL), F32)], axis=0)
    xb = x.astype(BF16)

    ca_p, ca_s, cb_p, cb_s, dv_s = [], [], [], [], []
    kv_p = [[] for _ in range(N_GROUPS)]
    kv_s = [[] for _ in range(N_GROUPS)]
    y_a, y_b, y_c, y_d = (jnp.zeros((m_all, w), BF16) for w in (W_A, W_B, W_C, W_D))
    for l in range(DEPTH):
        h = _in_proj(xb, w_in, l, tiles["in_proj"], 512)
        y_a, st_a = _branch_a(h, bp, tp, conv_a_w, y_a, l, 512)
        y_b, st_b, w_gate_b = _branch_b(h, bp, tp, conv_b_w, conv_b_bias, ln_b_g, ln_b_b,
                                        w_in, y_b, l, 256)
        y_d, *wouts_b = _branch_d(h, bp, tp, ln_d_g, ln_d_b, w_s, b_s, wouts, y_d, l, 512)
        o0, l0, kr0 = _attn_group(h, bp, tp, cos_p, sin_p, 0, 16)
        o1, l1, kr1 = _attn_group(h, bp, tp, cos_p, sin_p, 1, 4)
        y_c, kr2 = _attn_group(h, bp, tp, cos_p, sin_p, 2, 1, others=(o0, l0, o1, l1),
                               y_init=y_c)
        ca_p.append(st_a)
        cb_p.append(st_b)
        for g, kr in enumerate((kr0, kr1, kr2)):
            kv_p[g].append(_kv_rows(kr, h, bp, tp, g))

        hist_a = jnp.pad(state_conv_a[l].transpose(1, 0, 2), ((0, 0), (0, pad), (0, 0)))
        hist_b = jnp.pad(state_conv_b[l].transpose(1, 0, 2), ((0, 0), (0, pad), (0, 0)))
        y_a, y_b, y_d, na, nb_, vn, qk_s = _sample_mix(
            h, mp, (y_a, y_b, y_d), hist_a, hist_b, conv_a_w, conv_b_w, conv_b_bias,
            ln_b_g, ln_b_b, ln_d_g, ln_d_b, w_s, b_s, cos_s, sin_s, l)
        h_s = h[mp:]
        y_c = _sample_attn(qk_s, h_s, caches, y_c, mp, l, bs)
        ca_s.append(na[:, :bs].transpose(1, 0, 2))
        cb_s.append(nb_[:, :bs].transpose(1, 0, 2))
        dv_s.append(vn[:bs].reshape(bs, ts, W_D))
        for g in range(N_GROUPS):
            c0 = g * W_C
            k = qk_s[:bs, W_QKV + c0:W_QKV + c0 + W_C]
            v = h_s[:bs, COL_V + c0:COL_V + c0 + W_C]
            kv_s[g].append(jnp.stack([k, v], axis=1).reshape(bs, ts, 2, HEADS_PER_GROUP, HEAD_DIM))

        merged, w_o_b = _gate_out(xb, w_gate_b, b_gate, (y_a, y_b, y_c, y_d), wouts_b, w_o, l,
                                  tiles["gate_out"], 256)
        z = _proj_res(merged, w_o_b, x, tiles["proj_res"], 512)
        if l + 1 < DEPTH:
            x, xb = _ln_rows(z, ln_g, ln_b, l, tiles["ln_rows"])
        else:
            y_p, y_s = _ln_rows_split(z, ln_g, ln_b, l, mp, tiles["ln_out"])

    return (y_p.reshape(bp, tp, D_MODEL), y_s[:bs].reshape(bs, ts, D_MODEL),
            jnp.stack(ca_p), jnp.stack(ca_s), jnp.stack(cb_p), jnp.stack(cb_s),
            jnp.stack(kv_p[0]), jnp.stack(kv_s[0]), jnp.stack(kv_p[1]), jnp.stack(kv_s[1]),
            jnp.stack(kv_p[2]), jnp.stack(kv_s[2]), jnp.stack(dv_s))
```
